```python
import math
import jax, jax.numpy as jnp
from jax import lax
import numpy as np

D_MODEL = 1024
BATCH = 16
SEQ = 2048
DEPTH = 2

N_MIXERS = 2
N_RWKV_LAYERS = (DEPTH + N_MIXERS - 1) // N_MIXERS
N_DSA_LAYERS = DEPTH // N_MIXERS

RW_HEAD_DIM = 64
RW_HEADS = D_MODEL // RW_HEAD_DIM
RW_DECAY_LORA = max(32, int(round(1.8 * D_MODEL ** 0.5 / 32)) * 32)
RW_AAA_LORA = max(32, int(round(1.8 * D_MODEL ** 0.5 / 32)) * 32)
RW_GATE_LORA = max(32, int(round(0.6 * D_MODEL ** 0.8 / 32)) * 32)
RW_GN_EPS = 64e-5

ATT_HEAD_DIM = 64
ATT_HEADS = D_MODEL // ATT_HEAD_DIM
KV_DIM = ATT_HEAD_DIM
IDX_HEADS = 8
IDX_DIM = 64
TOPK_MAX = 256
Q_BLOCK = 128
ROPE_THETA = 10000.0
Q_WIDTH = ATT_HEADS * ATT_HEAD_DIM
DS_SPLITS = (Q_WIDTH, Q_WIDTH + KV_DIM, Q_WIDTH + 2 * KV_DIM,
             Q_WIDTH + 2 * KV_DIM + IDX_HEADS * IDX_DIM,
             Q_WIDTH + 2 * KV_DIM + IDX_HEADS * IDX_DIM + IDX_DIM)
DS_IN_DIM = DS_SPLITS[-1] + IDX_HEADS
IDX_W_SCALE = IDX_HEADS ** -0.5 * IDX_DIM ** -0.5

N_EXPERTS = 32
TOP_K = 4
D_EXPERT = D_MODEL
SWIGLU_LIMIT = 7.0
SWIGLU_ALPHA = 1.702
MOE_BLOCK = 256

LN_EPS = 1e-5
DEEPNORM_ALPHA = (2.0 * DEPTH) ** 0.25
DEEPNORM_BETA = (8.0 * DEPTH) ** -0.25

kernel_name = "hybrid_rwkv7_dsa_moe_deepnorm"


def layer_norm(x, g, b):
    xf = x.astype(jnp.float32)
    mu = jnp.mean(xf, axis=-1, keepdims=True)
    var = jnp.mean(jnp.square(xf - mu), axis=-1, keepdims=True)
    return ((xf - mu) * lax.rsqrt(var + LN_EPS)).astype(x.dtype) * g + b


def rope_tables(seq_len, dim):
    pos = jnp.arange(seq_len, dtype=jnp.float32)
    inv = 1.0 / (ROPE_THETA ** (jnp.arange(0, dim, 2, dtype=jnp.float32) / dim))
    ang = pos[:, None] * inv[None, :]
    return jnp.cos(ang), jnp.sin(ang)


def apply_rope(x, cos, sin):
    half = x.shape[-1] // 2
    shape = (1, x.shape[1]) + (1,) * (x.ndim - 3) + (half,)
    c = cos.reshape(shape).astype(x.dtype)
    s = sin.reshape(shape).astype(x.dtype)
    x1, x2 = x[..., :half], x[..., half:]
    return jnp.concatenate([x1 * c - x2 * s, x2 * c + x1 * s], axis=-1)


def rwkv7_step(S, inp):
    r_t, w_t, k_t, v_t, kk_t, a_t = inp
    s_kk = jnp.einsum('bhvk,bhk->bhv', S, kk_t)
    S = (S * w_t[:, :, None, :]
         - s_kk[..., None] * (kk_t * a_t)[:, :, None, :]
         + v_t[..., None] * k_t[:, :, None, :])
    y = jnp.einsum('bhvk,bhk->bhv', S, r_t)
    return S, y


def rwkv7_time_mix(x, mu, w_rkv, w0, w1, w2, a0, a1, a2, g1, g2, k_k, k_a, r_k,
                   lnx_g, lnx_b, w_o):
    B, T, C = x.shape
    H, N = RW_HEADS, RW_HEAD_DIM
    xx = jnp.pad(x[:, :-1], ((0, 0), (1, 0), (0, 0))) - x
    xs = x[None] + xx[None] * mu[:, None, None, :]
    r, k, v = jnp.einsum('pbtc,pcd->pbtd', xs[:3], w_rkv)
    xw, xa, xg = xs[3], xs[4], xs[5]
    w_log = -jax.nn.softplus(-(w0 + jnp.tanh(xw @ w1) @ w2)) - 0.5
    a = jax.nn.sigmoid(a0 + (xa @ a1) @ a2)
    g = jax.nn.sigmoid(xg @ g1) @ g2
    kk = (k * k_k).reshape(B, T, H, N).astype(jnp.float32)
    kk = kk / jnp.maximum(jnp.linalg.norm(kk, axis=-1, keepdims=True), 1e-12)
    k = k * (1.0 + (a - 1.0) * k_a)
    decay = jnp.exp(-jnp.exp(w_log.astype(jnp.float32)))

    def heads_tm(z):
        return z.reshape(B, T, H, N).astype(jnp.float32).transpose(1, 0, 2, 3)

    seq_in = (heads_tm(r), heads_tm(decay), heads_tm(k), heads_tm(v),
              kk.transpose(1, 0, 2, 3), heads_tm(a))
    S0 = jnp.zeros((B, H, N, N), jnp.float32)
    _, y = lax.scan(rwkv7_step, S0, seq_in)
    y = y.transpose(1, 0, 2, 3)
    ym = jnp.mean(y, axis=-1, keepdims=True)
    yv = jnp.mean(jnp.square(y - ym), axis=-1, keepdims=True)
    y = ((y - ym) * lax.rsqrt(yv + RW_GN_EPS)).reshape(B, T, C).astype(x.dtype) * lnx_g + lnx_b
    rh = r.reshape(B, T, H, N)
    kh = k.reshape(B, T, H, N)
    bonus = jnp.sum(rh * kh * r_k, axis=-1, keepdims=True) * v.reshape(B, T, H, N)
    y = y + bonus.reshape(B, T, C)
    return (y * g) @ w_o


def dsa_attention(x, w_in, idx_ln_g, idx_ln_b, w_o, cos, sin):
    B, T, C = x.shape
    proj = x @ w_in
    q, k, v, qi, ki, wi = jnp.split(proj, DS_SPLITS, axis=-1)
    q = apply_rope(q.reshape(B, T, ATT_HEADS, ATT_HEAD_DIM), cos, sin)
    k = apply_rope(k, cos, sin)
    qi = apply_rope(qi.reshape(B, T, IDX_HEADS, IDX_DIM), cos, sin)
    ki = apply_rope(layer_norm(ki, idx_ln_g, idx_ln_b), cos, sin)
    wi = wi * IDX_W_SCALE
    k_sel = min(TOPK_MAX, T // 4)
    nb = T // Q_BLOCK
    scale = ATT_HEAD_DIM ** -0.5
    kpos = jnp.arange(T)
    ki32 = ki.astype(jnp.float32)

    def to_blocks(z):
        return z.reshape((B, nb, Q_BLOCK) + z.shape[2:]).swapaxes(0, 1)

    def attend_block(args):
        qb, qib, wib, blk = args
        qpos = blk * Q_BLOCK + jnp.arange(Q_BLOCK)
        s = jnp.einsum('bqhd,bsd->bqhs', qib.astype(jnp.float32), ki32)
        s = jnp.einsum('bqhs,bqh->bqs', jax.nn.relu(s), wib.astype(jnp.float32))
        causal = kpos[None, :] <= qpos[:, None]
        s = jnp.where(causal[None], s, -jnp.inf)
        _, idx = lax.top_k(s, k_sel)
        valid = idx <= qpos[None, :, None]
        kg = jax.vmap(lambda src, ix: src[ix])(k, idx)
        vg = jax.vmap(lambda src, ix: src[ix])(v, idx)
        logits = jnp.einsum('bqhd,bqkd->bqhk', qb, kg).astype(jnp.float32) * scale
        logits = jnp.where(valid[:, :, None, :], logits, -jnp.inf)
        p = jax.nn.softmax(logits, axis=-1).astype(vg.dtype)
        return jnp.einsum('bqhk,bqkd->bqhd', p, vg)

    out = lax.map(attend_block, (to_blocks(q), to_blocks(qi), to_blocks(wi), jnp.arange(nb)))
    out = out.swapaxes(0, 1).reshape(B, T, Q_WIDTH)
    return out @ w_o


def moe_ffn(h, w_router, b_router, w_gu, b_gu, w_down, b_down):
    B, T, C = h.shape
    n_tok = B * T
    xt = h.reshape(n_tok, C)
    logits = (xt @ w_router + b_router).astype(jnp.float32)
    top_logit, top_e = lax.top_k(logits, TOP_K)
    gates = jax.nn.softmax(top_logit, axis=-1).astype(h.dtype)
    n_assign = n_tok * TOP_K
    e_flat = top_e.reshape(n_assign)
    order = jnp.argsort(e_flat)
    e_sorted = e_flat[order]
    tok_sorted = order // TOP_K
    counts = jnp.bincount(e_flat, length=N_EXPERTS)
    grp_start = jnp.cumsum(counts) - counts
    padded = (counts + MOE_BLOCK - 1) // MOE_BLOCK * MOE_BLOCK
    pad_end = jnp.cumsum(padded)
    pad_start = pad_end - padded
    dest = pad_start[e_sorted] + jnp.arange(n_assign) - grp_start[e_sorted]
    n_blocks = -(-(n_assign + N_EXPERTS * (MOE_BLOCK - 1)) // MOE_BLOCK)
    rows_tok = jnp.zeros((n_blocks * MOE_BLOCK,), jnp.int32).at[dest].set(tok_sorted)
    blk_e = jnp.minimum(jnp.searchsorted(pad_end, jnp.arange(n_blocks) * MOE_BLOCK, side='right'),
                        N_EXPERTS - 1)
    x_blocks = xt[rows_tok].reshape(n_blocks, MOE_BLOCK, C)

    def expert_block(args):
        xb, e = args
        gu = xb @ w_gu[e] + b_gu[e]
        glu = jnp.minimum(gu[:, :D_EXPERT], SWIGLU_LIMIT)
        lin = jnp.clip(gu[:, D_EXPERT:], -SWIGLU_LIMIT, SWIGLU_LIMIT)
        act = glu * jax.nn.sigmoid(SWIGLU_ALPHA * glu) * (lin + 1.0)
        return act @ w_down[e] + b_down[e]

    y_rows = lax.map(expert_block, (x_blocks, blk_e)).reshape(n_blocks * MOE_BLOCK, C)
    row_of_assign = jnp.zeros((n_assign,), jnp.int32).at[order].set(dest)
    y_assign = y_rows[row_of_assign].reshape(n_tok, TOP_K, C)
    out = jnp.einsum('nk,nkc->nc', gates, y_assign)
    return out.reshape(B, T, C)


def setup_inputs(seed: int = 0) -> dict:
    key = jax.random.key(seed)
    ks = iter(jax.random.split(key, 64))
    C, NA, NB, L, E, F = D_MODEL, N_RWKV_LAYERS, N_DSA_LAYERS, DEPTH, N_EXPERTS, D_EXPERT

    def nrm(shape, scale):
        return jax.random.normal(next(ks), shape, jnp.float32) * scale

    def uni(shape, lo, hi):
        return jax.random.uniform(next(ks), shape, jnp.float32, lo, hi)

    return {
        "x": nrm((BATCH, SEQ, C), 1.0),
        "rw_mu": uni((NA, 6, C), 0.0, 1.0),
        "rw_w_rkv": nrm((NA, 3, C, C), C ** -0.5),
        "rw_w0": uni((NA, C), -5.0, 0.0),
        "rw_w1": nrm((NA, C, RW_DECAY_LORA), C ** -0.5),
        "rw_w2": nrm((NA, RW_DECAY_LORA, C), 0.1 * RW_DECAY_LORA ** -0.5),
        "rw_a0": nrm((NA, C), 0.1),
        "rw_a1": nrm((NA, C, RW_AAA_LORA), C ** -0.5),
        "rw_a2": nrm((NA, RW_AAA_LORA, C), 0.5 * RW_AAA_LORA ** -0.5),
        "rw_g1": nrm((NA, C, RW_GATE_LORA), C ** -0.5),
        "rw_g2": nrm((NA, RW_GATE_LORA, C), RW_GATE_LORA ** -0.5),
        "rw_k_k": 0.85 + nrm((NA, C), 0.02),
        "rw_k_a": 1.0 + nrm((NA, C), 0.02),
        "rw_r_k": -0.04 + nrm((NA, RW_HEADS, RW_HEAD_DIM), 0.02),
        "rw_lnx_g": 1.0 + nrm((NA, C), 0.02),
        "rw_lnx_b": nrm((NA, C), 0.02),
        "rw_w_o": nrm((NA, C, C), C ** -0.5 * DEEPNORM_BETA),
        "ds_w_in": nrm((NB, C, DS_IN_DIM), C ** -0.5),
        "ds_idx_ln_g": 1.0 + nrm((NB, IDX_DIM), 0.02),
        "ds_idx_ln_b": nrm((NB, IDX_DIM), 0.02),
        "ds_w_o": nrm((NB, Q_WIDTH, C), Q_WIDTH ** -0.5 * DEEPNORM_BETA),
        "ln_mix_g": 1.0 + nrm((L, C), 0.02),
        "ln_mix_b": nrm((L, C), 0.02),
        "moe_w_router": nrm((L, C, E), C ** -0.5),
        "moe_b_router": nrm((L, E), 0.01),
        "moe_w_gu": nrm((L, E, C, 2 * F), C ** -0.5),
        "moe_b_gu": nrm((L, E, 2 * F), 0.02),
        "moe_w_down": nrm((L, E, F, C), F ** -0.5 * DEEPNORM_BETA),
        "moe_b_down": nrm((L, E, C), 0.02),
        "ln_ffn_g": 1.0 + nrm((L, C), 0.02),
        "ln_ffn_b": nrm((L, C), 0.02),
    }


def reference(x, rw_mu, rw_w_rkv, rw_w0, rw_w1, rw_w2, rw_a0, rw_a1, rw_a2, rw_g1, rw_g2,
              rw_k_k, rw_k_a, rw_r_k, rw_lnx_g, rw_lnx_b, rw_w_o,
              ds_w_in, ds_idx_ln_g, ds_idx_ln_b, ds_w_o,
              ln_mix_g, ln_mix_b, moe_w_router, moe_b_router, moe_w_gu, moe_b_gu,
              moe_w_down, moe_b_down, ln_ffn_g, ln_ffn_b):
    T = x.shape[1]
    cos, sin = rope_tables(T, ATT_HEAD_DIM)
    for i in range(DEPTH):
        j = i // N_MIXERS
        if i % N_MIXERS == 0:
            m = rwkv7_time_mix(x, rw_mu[j], rw_w_rkv[j], rw_w0[j], rw_w1[j], rw_w2[j],
                               rw_a0[j], rw_a1[j], rw_a2[j], rw_g1[j], rw_g2[j],
                               rw_k_k[j], rw_k_a[j], rw_r_k[j], rw_lnx_g[j], rw_lnx_b[j],
                               rw_w_o[j])
        else:
            m = dsa_attention(x, ds_w_in[j], ds_idx_ln_g[j], ds_idx_ln_b[j], ds_w_o[j], cos, sin)
        x = layer_norm(DEEPNORM_ALPHA * x + m, ln_mix_g[i], ln_mix_b[i])
        f = moe_ffn(x, moe_w_router[i], moe_b_router[i], moe_w_gu[i], moe_b_gu[i],
                    moe_w_down[i], moe_b_down[i])
        x = layer_norm(DEEPNORM_ALPHA * x + f, ln_ffn_g[i], ln_ffn_b[i])
    return x
```

```python
import functools
import math

import jax
import jax.numpy as jnp
from jax import lax
from jax.experimental import pallas as pl
from jax.experimental.pallas import tpu as pltpu

D_MODEL = 1024
HEAD_DIM = 64
N_HEADS = D_MODEL // HEAD_DIM
RW_GN_EPS = 64e-5
IDX_HEADS = 8
IDX_DIM = 64
TOPK_MAX = 256
ROPE_THETA = 10000.0
N_EXPERTS = 32
TOP_K = 4
SWIGLU_LIMIT = 7.0
SWIGLU_ALPHA = 1.702
LN_EPS = 1e-5
DEPTH = 2
DEEPNORM_ALPHA = (2.0 * DEPTH) ** 0.25
IDX_W_SCALE = IDX_HEADS ** -0.5 * IDX_DIM ** -0.5

V7X_LANES = 128
V7X_SUBLANES = 8
V7X_VMEM_LIMIT_BYTES = 56 * 1024 * 1024

SCAN_CHUNK = 64
ROW_TILE = 256
Q_TILE = 128
KEY_GROUP = 512
MOE_TILE = 256

F32 = jnp.float32
BF16 = jnp.bfloat16
INT_MIN = -2147483648


def _bf(x):
    return x.astype(BF16)


def _split2(x):
    hi = x.astype(BF16)
    lo = (x - hi.astype(F32)).astype(BF16)
    return hi, lo


def _dot(a, b):
    return jnp.dot(a, b, preferred_element_type=F32)


def _dot_nt(a, b):
    return lax.dot_general(a, b, (((1,), (1,)), ((), ())), preferred_element_type=F32)


def _dot_tn(a, b):
    return lax.dot_general(a, b, (((0,), (0,)), ((), ())), preferred_element_type=F32)


def _layer_norm(h, g, b):
    mu = jnp.mean(h, axis=-1, keepdims=True)
    d = h - mu
    var = jnp.mean(d * d, axis=-1, keepdims=True)
    return d * lax.rsqrt(var + LN_EPS) * g + b


def _sigmoid(z):
    return 1.0 / (1.0 + jnp.exp(-z))


def _params(*sem):
    return pltpu.CompilerParams(dimension_semantics=sem, vmem_limit_bytes=V7X_VMEM_LIMIT_BYTES)


def _full(shape):
    n = len(shape)
    return pl.BlockSpec(shape, lambda *_: (0,) * n)


def _rwkv_pre_kernel(x_ref, xp_ref, mu_ref, wrkv_ref, w1_ref, w2_ref, a1_ref, a2_ref, g1_ref, g2_ref,
                     vec_ref, bd_ref, r_ref, lw_ref, k_ref, v_ref, na_ref, b_ref, g_ref, *, tiles_per_seq):
    i = pl.program_id(0)
    x = x_ref[...]
    prev = xp_ref[V7X_SUBLANES - 1:V7X_SUBLANES, :]
    prev = jnp.where(i % tiles_per_seq == 0, 0.0, prev)
    rows = lax.broadcasted_iota(jnp.int32, x.shape, 0)
    xprev = jnp.where(rows == 0, prev, pltpu.roll(x, 1, 0))
    xx = xprev - x

    def mix(p):
        return x + xx * mu_ref[p:p + 1, :]

    w0 = vec_ref[0:1, :]
    a0 = vec_ref[1:2, :]
    k_k = vec_ref[2:3, :]
    k_a = vec_ref[3:4, :]

    r = _dot(_bf(mix(0)), wrkv_ref[0])
    k = _dot(_bf(mix(1)), wrkv_ref[1])
    v = _dot(_bf(mix(2)), wrkv_ref[2])

    zw = w0 + _dot(_bf(jnp.tanh(_dot(_bf(mix(3)), w1_ref[...]))), w2_ref[...])
    nz = -zw
    softplus = jnp.maximum(nz, 0.0) + jnp.log1p(jnp.exp(-jnp.abs(nz)))
    w_log = -softplus - 0.5
    lw = -jnp.exp(w_log)

    a = _sigmoid(a0 + _dot(_bf(_dot(_bf(mix(4)), a1_ref[...])), a2_ref[...]))
    g = _dot(_bf(_sigmoid(_dot(_bf(mix(5)), g1_ref[...]))), g2_ref[...])

    kk = k * k_k
    sq_hi, sq_lo = _split2(kk * kk)
    ss = _dot(sq_hi, bd_ref[...]) + _dot(sq_lo, bd_ref[...])
    kk = kk / jnp.maximum(jnp.sqrt(ss), 1e-12)
    k = k * (1.0 + (a - 1.0) * k_a)

    r_ref[...] = r
    lw_ref[...] = lw
    k_ref[...] = k
    v_ref[...] = v
    na_ref[...] = -kk
    b_ref[...] = kk * a
    g_ref[...] = g


def _rwkv_pre(x2d, seq_len, mu, w_rkv, w0, w1, w2, a0, a1, a2, g1, g2, k_k, k_a, bd):
    n_tok, c = x2d.shape
    tm = min(ROW_TILE, seq_len)
    assert seq_len % tm == 0 and tm % V7X_SUBLANES == 0
    vecs = jnp.zeros((V7X_SUBLANES, c), F32).at[0].set(w0).at[1].set(a0).at[2].set(k_k).at[3].set(k_a)
    mu8 = jnp.zeros((V7X_SUBLANES, c), F32).at[:6].set(mu)
    row = pl.BlockSpec((tm, c), lambda i: (i, 0))
    sub_per_tile = tm // V7X_SUBLANES
    prev = pl.BlockSpec((V7X_SUBLANES, c), lambda i: (jnp.maximum(i * sub_per_tile - 1, 0), 0))
    ws = [_bf(w_rkv), _bf(w1), _bf(w2), _bf(a1), _bf(a2), _bf(g1), _bf(g2)]
    out = jax.ShapeDtypeStruct((n_tok, c), F32)
    return pl.pallas_call(
        functools.partial(_rwkv_pre_kernel, tiles_per_seq=seq_len // tm),
        grid=(n_tok // tm,),
        in_specs=[row, prev, _full(mu8.shape)] + [_full(w.shape) for w in ws] + [_full(vecs.shape), _full(bd.shape)],
        out_specs=[row] * 7,
        out_shape=[out] * 7,
        compiler_params=_params("parallel"),
        name="rwkv_pre",
    )(x2d, x2d, mu8, *ws, vecs, bd)


def _rwkv_scan_kernel(r_ref, lw_ref, k_ref, v_ref, na_ref, b_ref, g_ref, vec_ref, o_ref, s_ref):
    L, C = r_ref.shape
    N = HEAD_DIM

    @pl.when(pl.program_id(1) == 0)
    def _():
        s_ref[...] = jnp.zeros_like(s_ref)

    rows = lax.broadcasted_iota(jnp.int32, (L, L), 0)
    cols = lax.broadcasted_iota(jnp.int32, (L, L), 1)
    incl = rows >= cols
    strict = rows > cols

    lw = lw_ref[...]
    tri = jnp.where(incl, 1.0, 0.0).astype(BF16)
    h1 = lw.astype(BF16)
    r1 = lw - h1.astype(F32)
    h2 = r1.astype(BF16)
    h3 = (r1 - h2.astype(F32)).astype(BF16)
    cs = _dot(tri, h1) + _dot(tri, h2) + _dot(tri, h3)
    cs_last = cs[L - 1:L, :]
    g_incl = jnp.exp(cs)
    g_excl = jnp.exp(cs - lw)
    g_inv = jnp.exp(-cs)
    g_tail = jnp.exp(cs_last - cs)
    g_last = jnp.exp(cs_last)

    r = r_ref[...]
    k = k_ref[...]
    v = v_ref[...]
    na = na_ref[...]
    b = b_ref[...]
    gate = g_ref[...]
    at = na * g_excl
    rt = r * g_incl
    bt = b * g_inv
    kt = k * g_inv
    bh = b * g_tail
    kh = k * g_tail
    rk = r * k * vec_ref[2:3, :]
    lnx_g = vec_ref[0:1, :]
    lnx_b = vec_ref[1:2, :]

    for h in range(C // N):
        sl = slice(h * N, (h + 1) * N)
        at_h, rt_h, v_h = at[:, sl], rt[:, sl], v[:, sl]
        gram = _dot_nt(_bf(jnp.concatenate([at_h, rt_h], axis=0)),
                       _bf(jnp.concatenate([bt[:, sl], kt[:, sl]], axis=0)))
        m_ab = jnp.where(strict, gram[:L, :L], 0.0)
        m_ak = jnp.where(strict, gram[:L, L:], 0.0)
        w_rb = jnp.where(incl, gram[L:, :L], 0.0)
        w_rk = jnp.where(incl, gram[L:, L:], 0.0)

        xs = jnp.concatenate([at_h, _dot(_bf(m_ak), _bf(v_h))], axis=1)
        p = m_ab
        levels = int(math.log2(L))
        for lvl in range(levels):
            if lvl < levels - 1:
                z = _dot(_bf(p), _bf(jnp.concatenate([p, xs], axis=1)))
                p = z[:, :L]
                xs = xs + z[:, L:]
            else:
                xs = xs + _dot(_bf(p), _bf(xs))
        a_p = xs[:, :N]
        u0 = xs[:, N:]

        zy = _dot(_bf(jnp.concatenate([w_rb, w_rk], axis=1)),
                  _bf(jnp.concatenate([xs, jnp.concatenate([jnp.zeros_like(v_h), v_h], axis=1)], axis=0)))
        r_p = rt_h + zy[:, :N]
        y0 = zy[:, N:]

        s0 = s_ref[h]
        uy = _dot_nt(_bf(jnp.concatenate([a_p, r_p], axis=0)), _bf(s0))
        u = uy[:L] + u0
        y = uy[L:] + y0
        s_ref[h] = s0 * g_last[:, sl] + _dot_tn(_bf(jnp.concatenate([u, v_h], axis=0)),
                                                _bf(jnp.concatenate([bh[:, sl], kh[:, sl]], axis=0)))

        ym = jnp.mean(y, axis=-1, keepdims=True)
        yd = y - ym
        yv = jnp.mean(yd * yd, axis=-1, keepdims=True)
        yn = yd * lax.rsqrt(yv + RW_GN_EPS) * lnx_g[:, sl] + lnx_b[:, sl]
        bonus = jnp.sum(rk[:, sl], axis=-1, keepdims=True) * v_h
        o_ref[:, sl] = (yn + bonus) * gate[:, sl]


def _rwkv_scan(r, lw, k, v, na, b, g, lnx_g, lnx_b, r_k, batch, seq_len):
    n_tok, c = r.shape
    L = SCAN_CHUNK
    assert seq_len % L == 0
    vecs = jnp.zeros((V7X_SUBLANES, c), F32).at[0].set(lnx_g).at[1].set(lnx_b).at[2].set(r_k.reshape(c))
    chunks = seq_len // L
    blk = pl.BlockSpec((L, c), lambda bi, ci: (bi * chunks + ci, 0))
    return pl.pallas_call(
        _rwkv_scan_kernel,
        grid=(batch, chunks),
        in_specs=[blk] * 7 + [_full(vecs.shape)],
        out_specs=blk,
        out_shape=jax.ShapeDtypeStruct((n_tok, c), F32),
        scratch_shapes=[pltpu.VMEM((c // HEAD_DIM, HEAD_DIM, HEAD_DIM), F32)],
        compiler_params=_params("parallel", "arbitrary"),
        name="rwkv_scan",
    )(r, lw, k, v, na, b, g, vecs)


def _proj_ln_kernel(x_ref, z_ref, w_ref, gb_ref, o_ref):
    m = _dot(_bf(z_ref[...]), w_ref[...])
    o_ref[...] = _layer_norm(DEEPNORM_ALPHA * x_ref[...] + m, gb_ref[0:1, :], gb_ref[1:2, :])


def _proj_ln(x2d, z2d, w, ln_g, ln_b):
    n_tok, c = x2d.shape
    tm = min(2 * ROW_TILE, n_tok)
    gb = jnp.zeros((V7X_SUBLANES, c), F32).at[0].set(ln_g).at[1].set(ln_b)
    row = pl.BlockSpec((tm, c), lambda i: (i, 0))
    return pl.pallas_call(
        _proj_ln_kernel,
        grid=(n_tok // tm,),
        in_specs=[row, row, _full(w.shape), _full(gb.shape)],
        out_specs=row,
        out_shape=jax.ShapeDtypeStruct((n_tok, c), F32),
        compiler_params=_params("parallel"),
        name="proj_ln",
    )(x2d, z2d, _bf(w), gb)


def _router_kernel(x_ref, wt_ref, b_ref, e_ref, gate_ref):
    x_hi, x_lo = _split2(x_ref[...])
    w_hi, w_lo = _split2(wt_ref[...])
    logits = _dot_nt(w_hi, x_hi) + _dot_nt(w_hi, x_lo) + _dot_nt(w_lo, x_hi) + b_ref[...]
    n_e = logits.shape[0]
    eid = lax.broadcasted_iota(jnp.int32, logits.shape, 0)
    tops, idxs = [], []
    for _ in range(TOP_K):
        m = jnp.max(logits, axis=0, keepdims=True)
        idx = jnp.min(jnp.where(logits == m, eid, n_e), axis=0, keepdims=True)
        tops.append(m)
        idxs.append(idx)
        logits = jnp.where(eid == idx, -jnp.inf, logits)
    ex = [jnp.exp(t - tops[0]) for t in tops]
    denom = ex[0] + ex[1] + ex[2] + ex[3]
    e_ref[...] = jnp.concatenate(idxs, axis=0)
    gate_ref[...] = jnp.concatenate([e / denom for e in ex], axis=0)


def _router(x2d, w_router, b_router):
    n_tok, c = x2d.shape
    tm = min(4 * ROW_TILE, n_tok)
    wt = w_router.T
    bcol = b_router.reshape(N_EXPERTS, 1)
    return pl.pallas_call(
        _router_kernel,
        grid=(n_tok // tm,),
        in_specs=[pl.BlockSpec((tm, c), lambda i: (i, 0)), _full(wt.shape), _full(bcol.shape)],
        out_specs=[pl.BlockSpec((TOP_K, tm), lambda i: (0, i))] * 2,
        out_shape=[jax.ShapeDtypeStruct((TOP_K, n_tok), jnp.int32), jax.ShapeDtypeStruct((TOP_K, n_tok), F32)],
        compiler_params=_params("parallel"),
        name="moe_router",
    )(x2d, wt, bcol)


def _expert_kernel(blk_e_ref, n_used_ref, x_ref, wgu_ref, bgu_ref, wd_ref, bd_ref, y_ref, wgu_bf, wd_bf):
    i = pl.program_id(0)
    e = blk_e_ref[i]
    changed = jnp.logical_or(i == 0, e != blk_e_ref[jnp.maximum(i - 1, 0)])

    @pl.when(changed)
    def _():
        wgu_bf[...] = _bf(wgu_ref[...])
        wd_bf[...] = _bf(wd_ref[...])

    @pl.when(i < n_used_ref[0])
    def _():
        f = wd_ref.shape[0]
        gu = _dot(x_ref[...], wgu_bf[...]) + bgu_ref[...]
        glu = jnp.minimum(gu[:, :f], SWIGLU_LIMIT)
        lin = jnp.clip(gu[:, f:], -SWIGLU_LIMIT, SWIGLU_LIMIT)
        act = glu * _sigmoid(SWIGLU_ALPHA * glu) * (lin + 1.0)
        y_ref[...] = _dot(_bf(act), wd_bf[...]) + bd_ref[...]

    @pl.when(i >= n_used_ref[0])
    def _():
        y_ref[...] = jnp.zeros_like(y_ref)


def _experts(x_rows, blk_e, n_used, w_gu, b_gu, w_down, b_down):
    n_rows, c = x_rows.shape
    n_e, _, f2 = w_gu.shape
    f = f2 // 2
    tb = MOE_TILE
    grid_spec = pltpu.PrefetchScalarGridSpec(
        num_scalar_prefetch=2,
        grid=(n_rows // tb,),
        in_specs=[
            pl.BlockSpec((tb, c), lambda i, be, nu: (i, 0)),
            pl.BlockSpec((None, c, f2), lambda i, be, nu: (be[i], 0, 0)),
            pl.BlockSpec((None, 1, f2), lambda i, be, nu: (be[i], 0, 0)),
            pl.BlockSpec((None, f, c), lambda i, be, nu: (be[i], 0, 0)),
            pl.BlockSpec((None, 1, c), lambda i, be, nu: (be[i], 0, 0)),
        ],
        out_specs=pl.BlockSpec((tb, c), lambda i, be, nu: (i, 0)),
        scratch_shapes=[pltpu.VMEM((c, f2), BF16), pltpu.VMEM((f, c), BF16)],
    )
    return pl.pallas_call(
        _expert_kernel,
        grid_spec=grid_spec,
        out_shape=jax.ShapeDtypeStruct((n_rows, c), F32),
        compiler_params=_params("arbitrary"),
        name="moe_experts",
    )(blk_e, n_used, x_rows, w_gu, b_gu.reshape(n_e, 1, f2), w_down, b_down.reshape(n_e, 1, c))


def _combine_ln_kernel(x_ref, y_ref, gate_ref, gb_ref, o_ref):
    c = x_ref.shape[1]
    acc = DEEPNORM_ALPHA * x_ref[...]
    for j in range(TOP_K):
        acc = acc + gate_ref[:, j:j + 1] * y_ref[:, j * c:(j + 1) * c]
    o_ref[...] = _layer_norm(acc, gb_ref[0:1, :], gb_ref[1:2, :])


def _combine_ln(x2d, y4, gates, ln_g, ln_b):
    n_tok, c = x2d.shape
    tm = min(ROW_TILE, n_tok)
    gb = jnp.zeros((V7X_SUBLANES, c), F32).at[0].set(ln_g).at[1].set(ln_b)
    return pl.pallas_call(
        _combine_ln_kernel,
        grid=(n_tok // tm,),
        in_specs=[pl.BlockSpec((tm, c), lambda i: (i, 0)), pl.BlockSpec((tm, TOP_K * c), lambda i: (i, 0)),
                  pl.BlockSpec((tm, TOP_K), lambda i: (i, 0)), _full(gb.shape)],
        out_specs=pl.BlockSpec((tm, c), lambda i: (i, 0)),
        out_shape=jax.ShapeDtypeStruct((n_tok, c), F32),
        compiler_params=_params("parallel"),
        name="moe_combine_ln",
    )(x2d, y4, gates, gb)


def _moe(x2d, w_router, b_router, w_gu, b_gu, w_down, b_down, ln_g, ln_b):
    n_tok, c = x2d.shape
    top_e, gates = _router(x2d, w_router, b_router)
    te = top_e.T
    tb = MOE_TILE
    onehot = jnp.sum(jax.nn.one_hot(te, N_EXPERTS, dtype=jnp.int32), axis=1)
    csum = jnp.cumsum(onehot, axis=0)
    counts = csum[-1]
    rank = jnp.take_along_axis(csum - onehot, te, axis=1)
    padded = (counts + tb - 1) // tb * tb
    pad_end = jnp.cumsum(padded)
    pad_start = pad_end - padded
    dest = pad_start[te] + rank
    n_rows = -(-(n_tok * TOP_K + N_EXPERTS * (tb - 1)) // tb) * tb
    tok = jnp.broadcast_to(jnp.arange(n_tok, dtype=jnp.int32)[:, None], (n_tok, TOP_K))
    rows_tok = jnp.zeros((n_rows,), jnp.int32).at[dest.reshape(-1)].set(tok.reshape(-1))
    blk_e = jnp.minimum(jnp.searchsorted(pad_end, jnp.arange(n_rows // tb, dtype=jnp.int32) * tb, side='right'),
                        N_EXPERTS - 1).astype(jnp.int32)
    n_used = (pad_end[-1] // tb).astype(jnp.int32).reshape(1)
    x_rows = _bf(x2d)[rows_tok]
    y_rows = _experts(x_rows, blk_e, n_used, w_gu, b_gu, w_down, b_down)
    y4 = y_rows[dest.reshape(-1)].reshape(n_tok, TOP_K * c)
    return _combine_ln(x2d, y4, gates.T, ln_g, ln_b)


def _rope(z, cos, sin_signed):
    half = HEAD_DIM // 2
    width = z.shape[-1]
    lane = lax.broadcasted_iota(jnp.int32, z.shape, z.ndim - 1)
    swapped = jnp.where(lane % HEAD_DIM < half, pltpu.roll(z, width - half, z.ndim - 1), pltpu.roll(z, half, z.ndim - 1))
    return z * cos + swapped * sin_signed


def _dsa_proj_kernel(x_ref, w_ref, cos_ref, sin_ref, gb_ref, q_ref, k_ref, v_ref, qi_ref, ki_ref, wi_ref):
    c = x_ref.shape[1]
    proj = _dot(_bf(x_ref[...]), w_ref[...])
    cos2 = cos_ref[...]
    sin2 = sin_ref[...]
    q_w = q_ref.shape[1]
    qi_w = qi_ref.shape[1]
    q = proj[:, :q_w]
    q_ref[...] = _bf(_rope(q, jnp.tile(cos2, (1, q_w // V7X_LANES)), jnp.tile(sin2, (1, q_w // V7X_LANES))))
    kv = proj[:, q_w:q_w + V7X_LANES]
    k_ref[...] = _bf(_rope(kv, cos2, sin2)[:, :HEAD_DIM])
    v_ref[...] = _bf(kv[:, HEAD_DIM:])
    o = q_w + V7X_LANES
    qi = proj[:, o:o + qi_w]
    qi_ref[...] = _rope(qi, jnp.tile(cos2, (1, qi_w // V7X_LANES)), jnp.tile(sin2, (1, qi_w // V7X_LANES)))
    o = o + qi_w
    kiw = proj[:, o:o + V7X_LANES]
    in_ki = lax.broadcasted_iota(jnp.int32, kiw.shape, 1) < IDX_DIM
    mu = jnp.sum(kiw, axis=-1, keepdims=True) * (1.0 / IDX_DIM)
    d = jnp.where(in_ki, kiw - mu, 0.0)
    var = jnp.sum(d * d, axis=-1, keepdims=True) * (1.0 / IDX_DIM)
    ki = d * lax.rsqrt(var + LN_EPS) * gb_ref[0:1, :] + gb_ref[1:2, :]
    ki_ref[...] = _rope(ki, cos2, sin2)[:, :IDX_DIM]
    o = o + V7X_LANES
    wi_ref[...] = proj[:, o:o + IDX_HEADS] * IDX_W_SCALE


def _dsa_proj(x2d, seq_len, w_in, idx_g, idx_b, cos2, sin2):
    n_tok, c = x2d.shape
    q_w = N_HEADS * HEAD_DIM
    qi_w = IDX_HEADS * IDX_DIM
    o_ki = q_w + 2 * HEAD_DIM + qi_w
    w_pad = jnp.concatenate([w_in[:, :o_ki + IDX_DIM], jnp.zeros((c, V7X_LANES - IDX_DIM), F32),
                             w_in[:, o_ki + IDX_DIM:], jnp.zeros((c, V7X_LANES - IDX_HEADS), F32)], axis=1)
    tm = min(ROW_TILE, seq_len)
    tiles = seq_len // tm
    gb = jnp.zeros((V7X_SUBLANES, V7X_LANES), F32).at[0, :IDX_DIM].set(idx_g).at[1, :IDX_DIM].set(idx_b)
    row = lambda w: pl.BlockSpec((tm, w), lambda i: (i, 0))
    tab = pl.BlockSpec((tm, V7X_LANES), lambda i: (i % tiles, 0))
    shapes = [(q_w, BF16), (HEAD_DIM, BF16), (HEAD_DIM, BF16), (qi_w, F32), (IDX_DIM, F32), (IDX_HEADS, F32)]
    return pl.pallas_call(
        _dsa_proj_kernel,
        grid=(n_tok // tm,),
        in_specs=[row(c), _full(w_pad.shape), tab, tab, _full(gb.shape)],
        out_specs=[row(w) for w, _ in shapes],
        out_shape=[jax.ShapeDtypeStruct((n_tok, w), dt) for w, dt in shapes],
        compiler_params=_params("parallel"),
        name="dsa_proj",
    )(x2d, _bf(w_pad), cos2, sin2, gb)


def _dsa_attn_kernel(x_ref, q_ref, qi_ref, wit_ref, ki_ref, k_ref, vt_ref, wo_ref, gb_ref, o_ref, *, q_block0, n_keys, k_sel):
    qb = q_block0 + pl.program_id(1)
    tq = q_ref.shape[0]
    nk = n_keys

    ki_hi, ki_lo = _split2(ki_ref[0:nk, :])
    ki3 = jnp.concatenate([ki_hi, ki_hi, ki_lo], axis=1)
    qi = qi_ref[...]
    s = jnp.zeros((nk, tq), F32)
    for h in range(IDX_HEADS):
        q_hi, q_lo = _split2(qi[:, h * IDX_DIM:(h + 1) * IDX_DIM])
        sh = _dot_nt(ki3, jnp.concatenate([q_hi, q_lo, q_hi], axis=1))
        s = s + jnp.maximum(sh, 0.0) * wit_ref[h:h + 1, :]

    key_pos = lax.broadcasted_iota(jnp.int32, (nk, tq), 0)
    q_pos = qb * tq + lax.broadcasted_iota(jnp.int32, (nk, tq), 1)
    causal = key_pos <= q_pos

    bits = pltpu.bitcast(s, jnp.int32)
    keys = jnp.where(causal, bits ^ ((bits >> 31) & jnp.int32(0x7FFFFFFF)), jnp.int32(INT_MIN))

    def bit_step(i, tau):
        cand = tau | jnp.left_shift(jnp.int32(1), 31 - i)
        cnt = jnp.sum((keys >= (cand ^ jnp.int32(INT_MIN))).astype(jnp.int32), axis=0, keepdims=True)
        return jnp.where(cnt >= k_sel, cand, tau)

    tau = lax.fori_loop(0, 32, bit_step, jnp.zeros((1, tq), jnp.int32)) ^ jnp.int32(INT_MIN)
    above = keys > tau
    tied = keys == tau
    room = (k_sel - jnp.sum(above.astype(jnp.int32), axis=0, keepdims=True)).astype(F32)
    tied_bf = jnp.where(tied, 1.0, 0.0).astype(BF16)
    cr = lax.broadcasted_iota(jnp.int32, (tq, tq), 0)
    cc = lax.broadcasted_iota(jnp.int32, (tq, tq), 1)
    strict_bf = jnp.where(cr > cc, 1.0, 0.0).astype(BF16)
    carry = jnp.zeros((1, tq), F32)
    before = []
    for ch in range(nk // tq):
        t = tied_bf[ch * tq:(ch + 1) * tq, :]
        before.append(_dot(strict_bf, t) + carry)
        carry = carry + jnp.sum(t.astype(F32), axis=0, keepdims=True)
    before = jnp.concatenate(before, axis=0)
    sel = jnp.logical_and(jnp.logical_or(above, jnp.logical_and(tied, before < room)), causal)

    scale = HEAD_DIM ** -0.5
    k = k_ref[0:nk, :]
    vt = vt_ref[:, 0:nk]
    q = q_ref[...]
    outs = []
    for h in range(q.shape[1] // HEAD_DIM):
        logits = _dot_nt(k, q[:, h * HEAD_DIM:(h + 1) * HEAD_DIM]) * scale
        logits = jnp.where(sel, logits, -1e30)
        m = jnp.max(logits, axis=0, keepdims=True)
        p = jnp.exp(logits - m)
        denom = jnp.sum(p, axis=0, keepdims=True)
        outs.append(_dot(vt, _bf(p)) / denom)
    att_t = jnp.concatenate(outs, axis=0)
    m_out = _dot_tn(_bf(att_t), wo_ref[...])
    o_ref[...] = _layer_norm(DEEPNORM_ALPHA * x_ref[...] + m_out, gb_ref[0:1, :], gb_ref[1:2, :])


def _dsa_attn(x2d, batch, seq_len, q, k, v, qi, ki, wi, w_o, ln_g, ln_b):
    n_tok, c = x2d.shape
    tq = Q_TILE
    k_sel = min(TOPK_MAX, seq_len // 4)
    kg = min(KEY_GROUP, seq_len)
    assert seq_len % kg == 0 and kg % tq == 0
    qb_per_group = kg // tq
    qb_per_seq = seq_len // tq
    vt = v.reshape(batch, seq_len, HEAD_DIM).transpose(0, 2, 1)
    wit = wi.reshape(batch, seq_len, IDX_HEADS).transpose(0, 2, 1)
    ki3 = ki.reshape(batch, seq_len, IDX_DIM)
    k3 = k.reshape(batch, seq_len, HEAD_DIM)
    gb = jnp.zeros((V7X_SUBLANES, c), F32).at[0].set(ln_g).at[1].set(ln_b)
    wo = _bf(w_o)
    outs = []
    for grp in range(seq_len // kg):
        q0 = grp * qb_per_group
        n_keys = (grp + 1) * kg
        row = lambda w: pl.BlockSpec((tq, w), lambda bi, j: (bi * qb_per_seq + q0 + j, 0))
        out = pl.pallas_call(
            functools.partial(_dsa_attn_kernel, q_block0=q0, n_keys=n_keys, k_sel=k_sel),
            grid=(batch, qb_per_group),
            in_specs=[row(c), row(q.shape[1]), row(qi.shape[1]),
                      pl.BlockSpec((None, IDX_HEADS, tq), lambda bi, j: (bi, 0, q0 + j)),
                      pl.BlockSpec((None, seq_len, IDX_DIM), lambda bi, j: (bi, 0, 0)),
                      pl.BlockSpec((None, seq_len, HEAD_DIM), lambda bi, j: (bi, 0, 0)),
                      pl.BlockSpec((None, HEAD_DIM, seq_len), lambda bi, j: (bi, 0, 0)),
                      _full(wo.shape), _full(gb.shape)],
            out_specs=pl.BlockSpec((None, tq, c), lambda bi, j: (bi, j, 0)),
            out_shape=jax.ShapeDtypeStruct((batch, kg, c), F32),
            compiler_params=_params("parallel", "parallel"),
            name=f"dsa_attn_{n_keys}",
        )(x2d, q, qi, wit, ki3, k3, vt, wo, gb)
        outs.append(out)
    return jnp.concatenate(outs, axis=1).reshape(n_tok, c)


def _rope_tables(seq_len):
    pos = jnp.arange(seq_len, dtype=F32)
    inv = 1.0 / (ROPE_THETA ** (jnp.arange(0, HEAD_DIM, 2, dtype=F32) / HEAD_DIM))
    ang = pos[:, None] * inv[None, :]
    cos, sin = jnp.cos(ang), jnp.sin(ang)
    cos2 = jnp.tile(jnp.concatenate([cos, cos], axis=1), (1, V7X_LANES // HEAD_DIM))
    sin2 = jnp.tile(jnp.concatenate([-sin, sin], axis=1), (1, V7X_LANES // HEAD_DIM))
    return cos2, sin2


def kernel(x, rw_mu, rw_w_rkv, rw_w0, rw_w1, rw_w2, rw_a0, rw_a1, rw_a2, rw_g1, rw_g2, rw_k_k, rw_k_a, rw_r_k, rw_lnx_g, rw_lnx_b, rw_w_o, ds_w_in, ds_idx_ln_g, ds_idx_ln_b, ds_w_o, ln_mix_g, ln_mix_b, moe_w_router, moe_b_router, moe_w_gu, moe_b_gu, moe_w_down, moe_b_down, ln_ffn_g, ln_ffn_b):
    batch, seq_len, c = x.shape
    assert c == D_MODEL
    h = x.reshape(batch * seq_len, c)
    cos2, sin2 = _rope_tables(seq_len)
    head_of = jnp.arange(c, dtype=jnp.int32) // HEAD_DIM
    bd = (head_of[:, None] == head_of[None, :]).astype(BF16)
    n_mixers = 2
    for i in range(DEPTH):
        j = i // n_mixers
        if i % n_mixers == 0:
            r, lw, k, v, na, b, g = _rwkv_pre(h, seq_len, rw_mu[j], rw_w_rkv[j], rw_w0[j], rw_w1[j], rw_w2[j],
                                              rw_a0[j], rw_a1[j], rw_a2[j], rw_g1[j], rw_g2[j], rw_k_k[j], rw_k_a[j], bd)
            z = _rwkv_scan(r, lw, k, v, na, b, g, rw_lnx_g[j], rw_lnx_b[j], rw_r_k[j], batch, seq_len)
            h = _proj_ln(h, z, rw_w_o[j], ln_mix_g[i], ln_mix_b[i])
        else:
            q, k, v, qi, ki, wi = _dsa_proj(h, seq_len, ds_w_in[j], ds_idx_ln_g[j], ds_idx_ln_b[j], cos2, sin2)
            h = _dsa_attn(h, batch, seq_len, q, k, v, qi, ki, wi, ds_w_o[j], ln_mix_g[i], ln_mix_b[i])
        h = _moe(h, moe_w_router[i], moe_b_router[i], moe_w_gu[i], moe_b_gu[i], moe_w_down[i], moe_b_down[i],
                 ln_ffn_g[i], ln_ffn_b[i])
    return h.reshape(batch, seq_len, c)
```

```python
import functools
import math

import jax
import jax.numpy as jnp
from jax import lax
from jax.experimental import pallas as pl
from jax.experimental.pallas import tpu as pltpu

D_MODEL = 1024
HEAD_DIM = 64
N_HEADS = D_MODEL // HEAD_DIM
RW_GN_EPS = 64e-5
IDX_HEADS = 8
IDX_DIM = 64
TOPK_MAX = 256
ROPE_THETA = 10000.0
N_EXPERTS = 32
TOP_K = 4
SWIGLU_LIMIT = 7.0
SWIGLU_ALPHA = 1.702
LN_EPS = 1e-5
DEPTH = 2
DEEPNORM_ALPHA = (2.0 * DEPTH) ** 0.25
IDX_W_SCALE = IDX_HEADS ** -0.5 * IDX_DIM ** -0.5

V7X_LANES = 128
V7X_SUBLANES = 8
V7X_VMEM_LIMIT_BYTES = 56 * 1024 * 1024

SCAN_CHUNK = 64
ROW_TILE = 256
Q_TILE = 128
KEY_GROUP = 512
MOE_TILE = 256

F32 = jnp.float32
BF16 = jnp.bfloat16
INT_MIN = -2147483648


def _bf(x):
    return x.astype(BF16)


def _split2(x):
    hi = x.astype(BF16)
    lo = (x - hi.astype(F32)).astype(BF16)
    return hi, lo


def _dot(a, b):
    return jnp.dot(a, b, preferred_element_type=F32)


def _dot_nt(a, b):
    return lax.dot_general(a, b, (((1,), (1,)), ((), ())), preferred_element_type=F32)


def _dot_tn(a, b):
    return lax.dot_general(a, b, (((0,), (0,)), ((), ())), preferred_element_type=F32)


def _layer_norm(h, g, b):
    mu = jnp.mean(h, axis=-1, keepdims=True)
    d = h - mu
    var = jnp.mean(d * d, axis=-1, keepdims=True)
    return d * lax.rsqrt(var + LN_EPS) * g + b


def _sigmoid(z):
    return 1.0 / (1.0 + jnp.exp(-z))


def _params(*sem):
    return pltpu.CompilerParams(dimension_semantics=sem, vmem_limit_bytes=V7X_VMEM_LIMIT_BYTES)


def _full(shape):
    n = len(shape)
    return pl.BlockSpec(shape, lambda *_: (0,) * n)


def _rwkv_pre_kernel(x_ref, xp_ref, mu_ref, wrkv_ref, w1_ref, w2_ref, a1_ref, a2_ref, g1_ref, g2_ref,
                     vec_ref, bd_ref, r_ref, lw_ref, k_ref, v_ref, na_ref, b_ref, g_ref, *, tiles_per_seq):
    i = pl.program_id(0)
    x = x_ref[...]
    prev = xp_ref[V7X_SUBLANES - 1:V7X_SUBLANES, :]
    prev = jnp.where(i % tiles_per_seq == 0, 0.0, prev)
    rows = lax.broadcasted_iota(jnp.int32, x.shape, 0)
    xprev = jnp.where(rows == 0, prev, pltpu.roll(x, 1, 0))
    xx = xprev - x

    def mix(p):
        return x + xx * mu_ref[p:p + 1, :]

    w0 = vec_ref[0:1, :]
    a0 = vec_ref[1:2, :]
    k_k = vec_ref[2:3, :]
    k_a = vec_ref[3:4, :]

    r = _dot(_bf(mix(0)), wrkv_ref[0])
    k = _dot(_bf(mix(1)), wrkv_ref[1])
    v = _dot(_bf(mix(2)), wrkv_ref[2])

    zw = w0 + _dot(_bf(jnp.tanh(_dot(_bf(mix(3)), w1_ref[...]))), w2_ref[...])
    nz = -zw
    softplus = jnp.maximum(nz, 0.0) + jnp.log1p(jnp.exp(-jnp.abs(nz)))
    w_log = -softplus - 0.5
    lw = -jnp.exp(w_log)

    a = _sigmoid(a0 + _dot(_bf(_dot(_bf(mix(4)), a1_ref[...])), a2_ref[...]))
    g = _dot(_bf(_sigmoid(_dot(_bf(mix(5)), g1_ref[...]))), g2_ref[...])

    kk = k * k_k
    sq_hi, sq_lo = _split2(kk * kk)
    ss = _dot(sq_hi, bd_ref[...]) + _dot(sq_lo, bd_ref[...])
    kk = kk / jnp.maximum(jnp.sqrt(ss), 1e-12)
    k = k * (1.0 + (a - 1.0) * k_a)

    r_ref[...] = r
    lw_ref[...] = lw
    k_ref[...] = k
    v_ref[...] = v
    na_ref[...] = -kk
    b_ref[...] = kk * a
    g_ref[...] = g


def _rwkv_pre(x2d, seq_len, mu, w_rkv, w0, w1, w2, a0, a1, a2, g1, g2, k_k, k_a, bd):
    n_tok, c = x2d.shape
    tm = min(ROW_TILE, seq_len)
    assert seq_len % tm == 0 and tm % V7X_SUBLANES == 0
    vecs = jnp.zeros((V7X_SUBLANES, c), F32).at[0].set(w0).at[1].set(a0).at[2].set(k_k).at[3].set(k_a)
    mu8 = jnp.zeros((V7X_SUBLANES, c), F32).at[:6].set(mu)
    row = pl.BlockSpec((tm, c), lambda i: (i, 0))
    sub_per_tile = tm // V7X_SUBLANES
    prev = pl.BlockSpec((V7X_SUBLANES, c), lambda i: (jnp.maximum(i * sub_per_tile - 1, 0), 0))
    ws = [_bf(w_rkv), _bf(w1), _bf(w2), _bf(a1), _bf(a2), _bf(g1), _bf(g2)]
    out = jax.ShapeDtypeStruct((n_tok, c), F32)
    return pl.pallas_call(
        functools.partial(_rwkv_pre_kernel, tiles_per_seq=seq_len // tm),
        grid=(n_tok // tm,),
        in_specs=[row, prev, _full(mu8.shape)] + [_full(w.shape) for w in ws] + [_full(vecs.shape), _full(bd.shape)],
        out_specs=[row] * 7,
        out_shape=[out] * 7,
        compiler_params=_params("parallel"),
        name="rwkv_pre",
    )(x2d, x2d, mu8, *ws, vecs, bd)


def _rwkv_scan_kernel(r_ref, lw_ref, k_ref, v_ref, na_ref, b_ref, g_ref, vec_ref, o_ref, s_ref):
    L, C = r_ref.shape
    N = HEAD_DIM

    @pl.when(pl.program_id(1) == 0)
    def _():
        s_ref[...] = jnp.zeros_like(s_ref)

    rows = lax.broadcasted_iota(jnp.int32, (L, L), 0)
    cols = lax.broadcasted_iota(jnp.int32, (L, L), 1)
    incl = rows >= cols
    strict = rows > cols

    lw = lw_ref[...]
    tri = jnp.where(incl, 1.0, 0.0).astype(BF16)
    h1 = lw.astype(BF16)
    r1 = lw - h1.astype(F32)
    h2 = r1.astype(BF16)
    h3 = (r1 - h2.astype(F32)).astype(BF16)
    cs = _dot(tri, h1) + _dot(tri, h2) + _dot(tri, h3)
    cs_last = cs[L - 1:L, :]
    g_incl = jnp.exp(cs)
    g_excl = jnp.exp(cs - lw)
    g_inv = jnp.exp(-cs)
    g_tail = jnp.exp(cs_last - cs)
    g_last = jnp.exp(cs_last)

    r = r_ref[...]
    k = k_ref[...]
    v = v_ref[...]
    na = na_ref[...]
    b = b_ref[...]
    gate = g_ref[...]
    at = na * g_excl
    rt = r * g_incl
    bt = b * g_inv
    kt = k * g_inv
    bh = b * g_tail
    kh = k * g_tail
    rk = r * k * vec_ref[2:3, :]
    lnx_g = vec_ref[0:1, :]
    lnx_b = vec_ref[1:2, :]

    heads = range(C // N)
    sls = [slice(h * N, (h + 1) * N) for h in heads]
    at_h = [at[:, sl] for sl in sls]
    rt_h = [rt[:, sl] for sl in sls]
    v_h = [v[:, sl] for sl in sls]
    v_bf = [_bf(x) for x in v_h]
    gram = [_dot_nt(_bf(jnp.concatenate([at_h[h], rt_h[h]], axis=0)),
                    _bf(jnp.concatenate([bt[:, sls[h]], kt[:, sls[h]]], axis=0))) for h in heads]
    mv = [_dot(_bf(jnp.where(strict, gram[h][:L, L:], 0.0)), v_bf[h]) for h in heads]
    xs = [jnp.concatenate([at_h[h], mv[h]], axis=1) for h in heads]
    p = [jnp.where(strict, gram[h][:L, :L], 0.0) for h in heads]
    levels = int(math.log2(L))
    for lvl in range(levels):
        if lvl < levels - 1:
            z = [_dot(_bf(p[h]), _bf(jnp.concatenate([p[h], xs[h]], axis=1))) for h in heads]
            p = [z[h][:, :L] for h in heads]
            xs = [xs[h] + z[h][:, L:] for h in heads]
        else:
            xs = [xs[h] + _dot(_bf(p[h]), _bf(xs[h])) for h in heads]
    rows2 = lax.broadcasted_iota(jnp.int32, (L, 2 * L), 0)
    cols2 = lax.broadcasted_iota(jnp.int32, (L, 2 * L), 1)
    incl2 = rows2 >= jnp.where(cols2 >= L, cols2 - L, cols2)
    w_r = [_bf(jnp.where(incl2, gram[h][L:, :], 0.0)) for h in heads]
    zy = [_dot(w_r[h], _bf(jnp.concatenate([xs[h], jnp.concatenate([jnp.zeros_like(v_h[h]), v_h[h]], axis=1)], axis=0)))
          for h in heads]
    s0 = [s_ref[h] for h in heads]
    uy = [_dot_nt(_bf(jnp.concatenate([xs[h][:, :N], rt_h[h] + zy[h][:, :N]], axis=0)), _bf(s0[h])) for h in heads]
    u = [uy[h][:L] + xs[h][:, N:] for h in heads]
    y = [uy[h][L:] + zy[h][:, N:] for h in heads]
    for h in heads:
        s_ref[h] = s0[h] * g_last[:, sls[h]] + _dot_tn(_bf(jnp.concatenate([u[h], v_h[h]], axis=0)),
                                                      _bf(jnp.concatenate([bh[:, sls[h]], kh[:, sls[h]]], axis=0)))
    for h in heads:
        sl = sls[h]
        ym = jnp.mean(y[h], axis=-1, keepdims=True)
        yd = y[h] - ym
        yv = jnp.mean(yd * yd, axis=-1, keepdims=True)
        yn = yd * lax.rsqrt(yv + RW_GN_EPS) * lnx_g[:, sl] + lnx_b[:, sl]
        bonus = jnp.sum(rk[:, sl], axis=-1, keepdims=True) * v_h[h]
        o_ref[:, sl] = (yn + bonus) * gate[:, sl]


def _rwkv_scan(r, lw, k, v, na, b, g, lnx_g, lnx_b, r_k, batch, seq_len):
    n_tok, c = r.shape
    L = SCAN_CHUNK
    assert seq_len % L == 0
    vecs = jnp.zeros((V7X_SUBLANES, c), F32).at[0].set(lnx_g).at[1].set(lnx_b).at[2].set(r_k.reshape(c))
    chunks = seq_len // L
    blk = pl.BlockSpec((L, c), lambda bi, ci: (bi * chunks + ci, 0))
    return pl.pallas_call(
        _rwkv_scan_kernel,
        grid=(batch, chunks),
        in_specs=[blk] * 7 + [_full(vecs.shape)],
        out_specs=blk,
        out_shape=jax.ShapeDtypeStruct((n_tok, c), F32),
        scratch_shapes=[pltpu.VMEM((c // HEAD_DIM, HEAD_DIM, HEAD_DIM), F32)],
        compiler_params=_params("parallel", "arbitrary"),
        name="rwkv_scan",
    )(r, lw, k, v, na, b, g, vecs)


def _proj_ln_kernel(x_ref, z_ref, w_ref, gb_ref, o_ref):
    m = _dot(_bf(z_ref[...]), w_ref[...])
    o_ref[...] = _layer_norm(DEEPNORM_ALPHA * x_ref[...] + m, gb_ref[0:1, :], gb_ref[1:2, :])


def _proj_ln(x2d, z2d, w, ln_g, ln_b):
    n_tok, c = x2d.shape
    tm = min(2 * ROW_TILE, n_tok)
    gb = jnp.zeros((V7X_SUBLANES, c), F32).at[0].set(ln_g).at[1].set(ln_b)
    row = pl.BlockSpec((tm, c), lambda i: (i, 0))
    return pl.pallas_call(
        _proj_ln_kernel,
        grid=(n_tok // tm,),
        in_specs=[row, row, _full(w.shape), _full(gb.shape)],
        out_specs=row,
        out_shape=jax.ShapeDtypeStruct((n_tok, c), F32),
        compiler_params=_params("parallel"),
        name="proj_ln",
    )(x2d, z2d, _bf(w), gb)


def _router_kernel(x_ref, wt_ref, b_ref, e_ref, gate_ref):
    x_hi, x_lo = _split2(x_ref[...])
    w_hi, w_lo = _split2(wt_ref[...])
    logits = _dot_nt(w_hi, x_hi) + _dot_nt(w_hi, x_lo) + _dot_nt(w_lo, x_hi) + b_ref[...]
    n_e = logits.shape[0]
    eid = lax.broadcasted_iota(jnp.int32, logits.shape, 0)
    tops, idxs = [], []
    for _ in range(TOP_K):
        m = jnp.max(logits, axis=0, keepdims=True)
        idx = jnp.min(jnp.where(logits == m, eid, n_e), axis=0, keepdims=True)
        tops.append(m)
        idxs.append(idx)
        logits = jnp.where(eid == idx, -jnp.inf, logits)
    ex = [jnp.exp(t - tops[0]) for t in tops]
    denom = ex[0] + ex[1] + ex[2] + ex[3]
    e_ref[...] = jnp.concatenate(idxs, axis=0)
    gate_ref[...] = jnp.concatenate([e / denom for e in ex], axis=0)


def _router(x2d, w_router, b_router):
    n_tok, c = x2d.shape
    tm = min(4 * ROW_TILE, n_tok)
    wt = w_router.T
    bcol = b_router.reshape(N_EXPERTS, 1)
    return pl.pallas_call(
        _router_kernel,
        grid=(n_tok // tm,),
        in_specs=[pl.BlockSpec((tm, c), lambda i: (i, 0)), _full(wt.shape), _full(bcol.shape)],
        out_specs=[pl.BlockSpec((TOP_K, tm), lambda i: (0, i))] * 2,
        out_shape=[jax.ShapeDtypeStruct((TOP_K, n_tok), jnp.int32), jax.ShapeDtypeStruct((TOP_K, n_tok), F32)],
        compiler_params=_params("parallel"),
        name="moe_router",
    )(x2d, wt, bcol)


def _expert_kernel(blk_e_ref, n_used_ref, x_ref, wgu_ref, bgu_ref, wd_ref, bd_ref, y_ref, wgu_bf, wd_bf):
    i = pl.program_id(0)
    e = blk_e_ref[i]
    changed = jnp.logical_or(i == 0, e != blk_e_ref[jnp.maximum(i - 1, 0)])

    @pl.when(changed)
    def _():
        wgu_bf[...] = _bf(wgu_ref[...])
        wd_bf[...] = _bf(wd_ref[...])

    @pl.when(i < n_used_ref[0])
    def _():
        f = wd_ref.shape[0]
        gu = _dot(x_ref[...], wgu_bf[...]) + bgu_ref[...]
        glu = jnp.minimum(gu[:, :f], SWIGLU_LIMIT)
        lin = jnp.clip(gu[:, f:], -SWIGLU_LIMIT, SWIGLU_LIMIT)
        act = glu * _sigmoid(SWIGLU_ALPHA * glu) * (lin + 1.0)
        y_ref[...] = _dot(_bf(act), wd_bf[...]) + bd_ref[...]

    @pl.when(i >= n_used_ref[0])
    def _():
        y_ref[...] = jnp.zeros_like(y_ref)


def _experts(x_rows, blk_e, n_used, w_gu, b_gu, w_down, b_down):
    n_rows, c = x_rows.shape
    n_e, _, f2 = w_gu.shape
    f = f2 // 2
    tb = MOE_TILE
    grid_spec = pltpu.PrefetchScalarGridSpec(
        num_scalar_prefetch=2,
        grid=(n_rows // tb,),
        in_specs=[
            pl.BlockSpec((tb, c), lambda i, be, nu: (i, 0)),
            pl.BlockSpec((None, c, f2), lambda i, be, nu: (be[i], 0, 0)),
            pl.BlockSpec((None, 1, f2), lambda i, be, nu: (be[i], 0, 0)),
            pl.BlockSpec((None, f, c), lambda i, be, nu: (be[i], 0, 0)),
            pl.BlockSpec((None, 1, c), lambda i, be, nu: (be[i], 0, 0)),
        ],
        out_specs=pl.BlockSpec((tb, c), lambda i, be, nu: (i, 0)),
        scratch_shapes=[pltpu.VMEM((c, f2), BF16), pltpu.VMEM((f, c), BF16)],
    )
    return pl.pallas_call(
        _expert_kernel,
        grid_spec=grid_spec,
        out_shape=jax.ShapeDtypeStruct((n_rows, c), F32),
        compiler_params=_params("arbitrary"),
        name="moe_experts",
    )(blk_e, n_used, x_rows, w_gu, b_gu.reshape(n_e, 1, f2), w_down, b_down.reshape(n_e, 1, c))


def _combine_ln_kernel(x_ref, y_ref, gate_ref, gb_ref, o_ref):
    acc = DEEPNORM_ALPHA * x_ref[...]
    for j in range(TOP_K):
        acc = acc + gate_ref[:, j:j + 1] * y_ref[j]
    o_ref[...] = _layer_norm(acc, gb_ref[0:1, :], gb_ref[1:2, :])


def _combine_ln(x2d, y4, gates, ln_g, ln_b):
    n_tok, c = x2d.shape
    tm = min(ROW_TILE, n_tok)
    gb = jnp.zeros((V7X_SUBLANES, c), F32).at[0].set(ln_g).at[1].set(ln_b)
    return pl.pallas_call(
        _combine_ln_kernel,
        grid=(n_tok // tm,),
        in_specs=[pl.BlockSpec((tm, c), lambda i: (i, 0)), pl.BlockSpec((TOP_K, tm, c), lambda i: (0, i, 0)),
                  pl.BlockSpec((tm, TOP_K), lambda i: (i, 0)), _full(gb.shape)],
        out_specs=pl.BlockSpec((tm, c), lambda i: (i, 0)),
        out_shape=jax.ShapeDtypeStruct((n_tok, c), F32),
        compiler_params=_params("parallel"),
        name="moe_combine_ln",
    )(x2d, y4, gates, gb)


def _moe(x2d, w_router, b_router, w_gu, b_gu, w_down, b_down, ln_g, ln_b):
    n_tok, c = x2d.shape
    top_e, gates = _router(x2d, w_router, b_router)
    te = top_e.T
    tb = MOE_TILE
    onehot = jnp.sum(jax.nn.one_hot(te, N_EXPERTS, dtype=jnp.int32), axis=1)
    csum = jnp.cumsum(onehot, axis=0)
    counts = csum[-1]
    rank = jnp.take_along_axis(csum - onehot, te, axis=1)
    padded = (counts + tb - 1) // tb * tb
    pad_end = jnp.cumsum(padded)
    pad_start = pad_end - padded
    dest = pad_start[te] + rank
    n_rows = -(-(n_tok * TOP_K + N_EXPERTS * (tb - 1)) // tb) * tb
    tok = jnp.broadcast_to(jnp.arange(n_tok, dtype=jnp.int32)[:, None], (n_tok, TOP_K))
    rows_tok = jnp.zeros((n_rows,), jnp.int32).at[dest.reshape(-1)].set(tok.reshape(-1))
    blk_e = jnp.minimum(jnp.searchsorted(pad_end, jnp.arange(n_rows // tb, dtype=jnp.int32) * tb, side='right'),
                        N_EXPERTS - 1).astype(jnp.int32)
    n_used = (pad_end[-1] // tb).astype(jnp.int32).reshape(1)
    x_rows = _bf(x2d)[rows_tok]
    y_rows = _experts(x_rows, blk_e, n_used, w_gu, b_gu, w_down, b_down)
    y4 = y_rows[dest.T.reshape(-1)].reshape(TOP_K, n_tok, c)
    return _combine_ln(x2d, y4, gates.T, ln_g, ln_b)


def _rope(z, cos, sin_signed):
    half = HEAD_DIM // 2
    width = z.shape[-1]
    lane = lax.broadcasted_iota(jnp.int32, z.shape, z.ndim - 1)
    swapped = jnp.where(lane % HEAD_DIM < half, pltpu.roll(z, width - half, z.ndim - 1), pltpu.roll(z, half, z.ndim - 1))
    return z * cos + swapped * sin_signed


def _dsa_proj_kernel(x_ref, w_ref, cos_ref, sin_ref, gb_ref, q_ref, k_ref, v_ref, qi_ref, ki_ref, wi_ref):
    c = x_ref.shape[1]
    proj = _dot(_bf(x_ref[...]), w_ref[...])
    cos2 = cos_ref[...]
    sin2 = sin_ref[...]
    q_w = q_ref.shape[1]
    qi_w = qi_ref.shape[1]
    q = proj[:, :q_w]
    q_ref[...] = _bf(_rope(q, jnp.tile(cos2, (1, q_w // V7X_LANES)), jnp.tile(sin2, (1, q_w // V7X_LANES))))
    kv = proj[:, q_w:q_w + V7X_LANES]
    k_ref[...] = _bf(_rope(kv, cos2, sin2)[:, :HEAD_DIM])
    v_ref[...] = _bf(kv[:, HEAD_DIM:])
    o = q_w + V7X_LANES
    qi = proj[:, o:o + qi_w]
    qi_ref[...] = _rope(qi, jnp.tile(cos2, (1, qi_w // V7X_LANES)), jnp.tile(sin2, (1, qi_w // V7X_LANES)))
    o = o + qi_w
    kiw = proj[:, o:o + V7X_LANES]
    in_ki = lax.broadcasted_iota(jnp.int32, kiw.shape, 1) < IDX_DIM
    mu = jnp.sum(kiw, axis=-1, keepdims=True) * (1.0 / IDX_DIM)
    d = jnp.where(in_ki, kiw - mu, 0.0)
    var = jnp.sum(d * d, axis=-1, keepdims=True) * (1.0 / IDX_DIM)
    ki = d * lax.rsqrt(var + LN_EPS) * gb_ref[0:1, :] + gb_ref[1:2, :]
    ki_ref[...] = _rope(ki, cos2, sin2)[:, :IDX_DIM]
    o = o + V7X_LANES
    wi_ref[...] = proj[:, o:o + IDX_HEADS] * IDX_W_SCALE


def _dsa_proj(x2d, seq_len, w_in, idx_g, idx_b, cos2, sin2):
    n_tok, c = x2d.shape
    q_w = N_HEADS * HEAD_DIM
    qi_w = IDX_HEADS * IDX_DIM
    o_ki = q_w + 2 * HEAD_DIM + qi_w
    w_pad = jnp.concatenate([w_in[:, :o_ki + IDX_DIM], jnp.zeros((c, V7X_LANES - IDX_DIM), F32),
                             w_in[:, o_ki + IDX_DIM:], jnp.zeros((c, V7X_LANES - IDX_HEADS), F32)], axis=1)
    tm = min(ROW_TILE, seq_len)
    tiles = seq_len // tm
    gb = jnp.zeros((V7X_SUBLANES, V7X_LANES), F32).at[0, :IDX_DIM].set(idx_g).at[1, :IDX_DIM].set(idx_b)
    row = lambda w: pl.BlockSpec((tm, w), lambda i: (i, 0))
    tab = pl.BlockSpec((tm, V7X_LANES), lambda i: (i % tiles, 0))
    shapes = [(q_w, BF16), (HEAD_DIM, BF16), (HEAD_DIM, BF16), (qi_w, F32), (IDX_DIM, F32), (IDX_HEADS, F32)]
    return pl.pallas_call(
        _dsa_proj_kernel,
        grid=(n_tok // tm,),
        in_specs=[row(c), _full(w_pad.shape), tab, tab, _full(gb.shape)],
        out_specs=[row(w) for w, _ in shapes],
        out_shape=[jax.ShapeDtypeStruct((n_tok, w), dt) for w, dt in shapes],
        compiler_params=_params("parallel"),
        name="dsa_proj",
    )(x2d, _bf(w_pad), cos2, sin2, gb)


def _dsa_attn_kernel(x_ref, q_ref, qi_ref, wit_ref, ki_ref, k_ref, vt_ref, wo_ref, gb_ref, o_ref, *, q_block0, n_keys, k_sel):
    qb = q_block0 + pl.program_id(1)
    tq = q_ref.shape[0]
    nk = n_keys

    ki_hi, ki_lo = _split2(ki_ref[0:nk, :])
    ki3 = jnp.concatenate([ki_hi, ki_hi, ki_lo], axis=1)
    qi = qi_ref[...]
    s = jnp.zeros((nk, tq), F32)
    for h in range(IDX_HEADS):
        q_hi, q_lo = _split2(qi[:, h * IDX_DIM:(h + 1) * IDX_DIM])
        sh = _dot_nt(ki3, jnp.concatenate([q_hi, q_lo, q_hi], axis=1))
        s = s + jnp.maximum(sh, 0.0) * wit_ref[h:h + 1, :]

    key_pos = lax.broadcasted_iota(jnp.int32, (nk, tq), 0)
    q_pos = qb * tq + lax.broadcasted_iota(jnp.int32, (nk, tq), 1)
    causal = key_pos <= q_pos

    bits = pltpu.bitcast(s, jnp.int32)
    keys = jnp.where(causal, bits ^ ((bits >> 31) & jnp.int32(0x7FFFFFFF)), jnp.int32(INT_MIN))

    def bit_step(i, tau):
        cand = tau | jnp.left_shift(jnp.int32(1), 31 - i)
        cnt = jnp.sum((keys >= (cand ^ jnp.int32(INT_MIN))).astype(jnp.int32), axis=0, keepdims=True)
        return jnp.where(cnt >= k_sel, cand, tau)

    tau = lax.fori_loop(0, 32, bit_step, jnp.zeros((1, tq), jnp.int32)) ^ jnp.int32(INT_MIN)
    above = keys > tau
    tied = keys == tau
    room = (k_sel - jnp.sum(above.astype(jnp.int32), axis=0, keepdims=True)).astype(F32)
    tied_bf = jnp.where(tied, 1.0, 0.0).astype(BF16)
    cr = lax.broadcasted_iota(jnp.int32, (tq, tq), 0)
    cc = lax.broadcasted_iota(jnp.int32, (tq, tq), 1)
    strict_bf = jnp.where(cr > cc, 1.0, 0.0).astype(BF16)
    carry = jnp.zeros((1, tq), F32)
    before = []
    for ch in range(nk // tq):
        t = tied_bf[ch * tq:(ch + 1) * tq, :]
        before.append(_dot(strict_bf, t) + carry)
        carry = carry + jnp.sum(t.astype(F32), axis=0, keepdims=True)
    before = jnp.concatenate(before, axis=0)
    sel = jnp.logical_and(jnp.logical_or(above, jnp.logical_and(tied, before < room)), causal)

    scale = HEAD_DIM ** -0.5
    k = k_ref[0:nk, :]
    vt = vt_ref[:, 0:nk]
    q = q_ref[...]
    outs = []
    for h in range(q.shape[1] // HEAD_DIM):
        logits = _dot_nt(k, q[:, h * HEAD_DIM:(h + 1) * HEAD_DIM]) * scale
        logits = jnp.where(sel, logits, -1e30)
        m = jnp.max(logits, axis=0, keepdims=True)
        p = jnp.exp(logits - m)
        denom = jnp.sum(p, axis=0, keepdims=True)
        outs.append(_dot(vt, _bf(p)) / denom)
    att_t = jnp.concatenate(outs, axis=0)
    m_out = _dot_tn(_bf(att_t), wo_ref[...])
    o_ref[...] = _layer_norm(DEEPNORM_ALPHA * x_ref[...] + m_out, gb_ref[0:1, :], gb_ref[1:2, :])


def _dsa_attn(x2d, batch, seq_len, q, k, v, qi, ki, wi, w_o, ln_g, ln_b):
    n_tok, c = x2d.shape
    tq = Q_TILE
    k_sel = min(TOPK_MAX, seq_len // 4)
    kg = min(KEY_GROUP, seq_len)
    assert seq_len % kg == 0 and kg % tq == 0
    qb_per_group = kg // tq
    qb_per_seq = seq_len // tq
    vt = v.reshape(batch, seq_len, HEAD_DIM).transpose(0, 2, 1)
    wit = wi.reshape(batch, seq_len, IDX_HEADS).transpose(0, 2, 1)
    ki3 = ki.reshape(batch, seq_len, IDX_DIM)
    k3 = k.reshape(batch, seq_len, HEAD_DIM)
    gb = jnp.zeros((V7X_SUBLANES, c), F32).at[0].set(ln_g).at[1].set(ln_b)
    wo = _bf(w_o)
    outs = []
    for grp in range(seq_len // kg):
        q0 = grp * qb_per_group
        n_keys = (grp + 1) * kg
        row = lambda w: pl.BlockSpec((tq, w), lambda bi, j: (bi * qb_per_seq + q0 + j, 0))
        out = pl.pallas_call(
            functools.partial(_dsa_attn_kernel, q_block0=q0, n_keys=n_keys, k_sel=k_sel),
            grid=(batch, qb_per_group),
            in_specs=[row(c), row(q.shape[1]), row(qi.shape[1]),
                      pl.BlockSpec((None, IDX_HEADS, tq), lambda bi, j: (bi, 0, q0 + j)),
                      pl.BlockSpec((None, seq_len, IDX_DIM), lambda bi, j: (bi, 0, 0)),
                      pl.BlockSpec((None, seq_len, HEAD_DIM), lambda bi, j: (bi, 0, 0)),
                      pl.BlockSpec((None, HEAD_DIM, seq_len), lambda bi, j: (bi, 0, 0)),
                      _full(wo.shape), _full(gb.shape)],
            out_specs=pl.BlockSpec((None, tq, c), lambda bi, j: (bi, j, 0)),
            out_shape=jax.ShapeDtypeStruct((batch, kg, c), F32),
            compiler_params=_params("parallel", "parallel"),
            name=f"dsa_attn_{n_keys}",
        )(x2d, q, qi, wit, ki3, k3, vt, wo, gb)
        outs.append(out)
    return jnp.concatenate(outs, axis=1).reshape(n_tok, c)


def _rope_tables(seq_len):
    pos = jnp.arange(seq_len, dtype=F32)
    inv = 1.0 / (ROPE_THETA ** (jnp.arange(0, HEAD_DIM, 2, dtype=F32) / HEAD_DIM))
    ang = pos[:, None] * inv[None, :]
    cos, sin = jnp.cos(ang), jnp.sin(ang)
    cos2 = jnp.tile(jnp.concatenate([cos, cos], axis=1), (1, V7X_LANES // HEAD_DIM))
    sin2 = jnp.tile(jnp.concatenate([-sin, sin], axis=1), (1, V7X_LANES // HEAD_DIM))
    return cos2, sin2


def kernel(x, rw_mu, rw_w_rkv, rw_w0, rw_w1, rw_w2, rw_a0, rw_a1, rw_a2, rw_g1, rw_g2, rw_k_k, rw_k_a, rw_r_k, rw_lnx_g, rw_lnx_b, rw_w_o, ds_w_in, ds_idx_ln_g, ds_idx_ln_b, ds_w_o, ln_mix_g, ln_mix_b, moe_w_router, moe_b_router, moe_w_gu, moe_b_gu, moe_w_down, moe_b_down, ln_ffn_g, ln_ffn_b):
    batch, seq_len, c = x.shape
    assert c == D_MODEL
    h = x.reshape(batch * seq_len, c)
    cos2, sin2 = _rope_tables(seq_len)
    head_of = jnp.arange(c, dtype=jnp.int32) // HEAD_DIM
    bd = (head_of[:, None] == head_of[None, :]).astype(BF16)
    n_mixers = 2
    for i in range(DEPTH):
        j = i // n_mixers
        if i % n_mixers == 0:
            r, lw, k, v, na, b, g = _rwkv_pre(h, seq_len, rw_mu[j], rw_w_rkv[j], rw_w0[j], rw_w1[j], rw_w2[j],
                                              rw_a0[j], rw_a1[j], rw_a2[j], rw_g1[j], rw_g2[j], rw_k_k[j], rw_k_a[j], bd)
            z = _rwkv_scan(r, lw, k, v, na, b, g, rw_lnx_g[j], rw_lnx_b[j], rw_r_k[j], batch, seq_len)
            h = _proj_ln(h, z, rw_w_o[j], ln_mix_g[i], ln_mix_b[i])
        else:
            q, k, v, qi, ki, wi = _dsa_proj(h, seq_len, ds_w_in[j], ds_idx_ln_g[j], ds_idx_ln_b[j], cos2, sin2)
            h = _dsa_attn(h, batch, seq_len, q, k, v, qi, ki, wi, ds_w_o[j], ln_mix_g[i], ln_mix_b[i])
        h = _moe(h, moe_w_router[i], moe_b_router[i], moe_w_gu[i], moe_b_gu[i], moe_w_down[i], moe_b_down[i],
                 ln_ffn_g[i], ln_ffn_b[i])
    return h.reshape(batch, seq_len, c)
```

```python
import functools
import math

import jax
import jax.numpy as jnp
from jax import lax
from jax.experimental import pallas as pl
from jax.experimental.pallas import tpu as pltpu

D_MODEL = 1024
HEAD_DIM = 64
N_HEADS = D_MODEL // HEAD_DIM
RW_GN_EPS = 64e-5
IDX_HEADS = 8
IDX_DIM = 64
TOPK_MAX = 256
ROPE_THETA = 10000.0
N_EXPERTS = 32
TOP_K = 4
SWIGLU_LIMIT = 7.0
SWIGLU_ALPHA = 1.702
LN_EPS = 1e-5
DEPTH = 2
DEEPNORM_ALPHA = (2.0 * DEPTH) ** 0.25
IDX_W_SCALE = IDX_HEADS ** -0.5 * IDX_DIM ** -0.5

V7X_LANES = 128
V7X_SUBLANES = 8
V7X_VMEM_LIMIT_BYTES = 56 * 1024 * 1024

SCAN_CHUNK = 64
ROW_TILE = 256
Q_TILE = 128
KEY_GROUP = 512
MOE_TILE = 512
MOE_ROUTER_TILE = 1024
DSA_HEAD_GROUP = 4

F32 = jnp.float32
BF16 = jnp.bfloat16
INT_MIN = -2147483648


def _bf(x):
    return x.astype(BF16)


def _split2(x):
    hi = x.astype(BF16)
    lo = (x - hi.astype(F32)).astype(BF16)
    return hi, lo


def _dot(a, b):
    return jnp.dot(a, b, preferred_element_type=F32)


def _dot_nt(a, b):
    return lax.dot_general(a, b, (((1,), (1,)), ((), ())), preferred_element_type=F32)


def _dot_tn(a, b):
    return lax.dot_general(a, b, (((0,), (0,)), ((), ())), preferred_element_type=F32)


def _layer_norm(h, g, b):
    mu = jnp.mean(h, axis=-1, keepdims=True)
    d = h - mu
    var = jnp.mean(d * d, axis=-1, keepdims=True)
    return d * lax.rsqrt(var + LN_EPS) * g + b


def _sigmoid(z):
    return 1.0 / (1.0 + jnp.exp(-z))


def _params(*sem):
    return pltpu.CompilerParams(dimension_semantics=sem, vmem_limit_bytes=V7X_VMEM_LIMIT_BYTES)


def _full(shape):
    n = len(shape)
    return pl.BlockSpec(shape, lambda *_: (0,) * n)


def _rwkv_pre_kernel(x_ref, xp_ref, mu_ref, wrkv_ref, w1_ref, w2_ref, a1_ref, a2_ref, g1_ref, g2_ref,
                     vec_ref, bd_ref, r_ref, lw_ref, k_ref, v_ref, na_ref, b_ref, g_ref, *, tiles_per_seq):
    i = pl.program_id(0)
    x = x_ref[...]
    prev = xp_ref[V7X_SUBLANES - 1:V7X_SUBLANES, :]
    prev = jnp.where(i % tiles_per_seq == 0, 0.0, prev)
    rows = lax.broadcasted_iota(jnp.int32, x.shape, 0)
    xprev = jnp.where(rows == 0, prev, pltpu.roll(x, 1, 0))
    xx = xprev - x

    def mix(p):
        return x + xx * mu_ref[p:p + 1, :]

    w0 = vec_ref[0:1, :]
    a0 = vec_ref[1:2, :]
    k_k = vec_ref[2:3, :]
    k_a = vec_ref[3:4, :]

    r = _dot(_bf(mix(0)), wrkv_ref[0])
    k = _dot(_bf(mix(1)), wrkv_ref[1])
    v = _dot(_bf(mix(2)), wrkv_ref[2])

    zw = w0 + _dot(_bf(jnp.tanh(_dot(_bf(mix(3)), w1_ref[...]))), w2_ref[...])
    nz = -zw
    softplus = jnp.maximum(nz, 0.0) + jnp.log1p(jnp.exp(-jnp.abs(nz)))
    w_log = -softplus - 0.5
    lw = -jnp.exp(w_log)

    a = _sigmoid(a0 + _dot(_bf(_dot(_bf(mix(4)), a1_ref[...])), a2_ref[...]))
    g = _dot(_bf(_sigmoid(_dot(_bf(mix(5)), g1_ref[...]))), g2_ref[...])

    kk = k * k_k
    sq_hi, sq_lo = _split2(kk * kk)
    gw = bd_ref.shape[0]
    ss = jnp.concatenate([_dot(sq_hi[:, c0:c0 + gw], bd_ref[...]) + _dot(sq_lo[:, c0:c0 + gw], bd_ref[...])
                          for c0 in range(0, kk.shape[1], gw)], axis=1)
    kk = kk / jnp.maximum(jnp.sqrt(ss), 1e-12)
    k = k * (1.0 + (a - 1.0) * k_a)

    r_ref[...] = r
    lw_ref[...] = lw
    k_ref[...] = k
    v_ref[...] = v
    na_ref[...] = -kk
    b_ref[...] = kk * a
    g_ref[...] = g


def _rwkv_pre(x2d, seq_len, mu, w_rkv, w0, w1, w2, a0, a1, a2, g1, g2, k_k, k_a, bd):
    n_tok, c = x2d.shape
    tm = min(ROW_TILE, seq_len)
    assert seq_len % tm == 0 and tm % V7X_SUBLANES == 0
    vecs = jnp.zeros((V7X_SUBLANES, c), F32).at[0].set(w0).at[1].set(a0).at[2].set(k_k).at[3].set(k_a)
    mu8 = jnp.zeros((V7X_SUBLANES, c), F32).at[:6].set(mu)
    row = pl.BlockSpec((tm, c), lambda i: (i, 0))
    sub_per_tile = tm // V7X_SUBLANES
    prev = pl.BlockSpec((V7X_SUBLANES, c), lambda i: (jnp.maximum(i * sub_per_tile - 1, 0), 0))
    ws = [_bf(w_rkv), _bf(w1), _bf(w2), _bf(a1), _bf(a2), _bf(g1), _bf(g2)]
    out = jax.ShapeDtypeStruct((n_tok, c), F32)
    return pl.pallas_call(
        functools.partial(_rwkv_pre_kernel, tiles_per_seq=seq_len // tm),
        grid=(n_tok // tm,),
        in_specs=[row, prev, _full(mu8.shape)] + [_full(w.shape) for w in ws] + [_full(vecs.shape), _full(bd.shape)],
        out_specs=[row] * 7,
        out_shape=[out] * 7,
        compiler_params=_params("parallel"),
        name="rwkv_pre",
    )(x2d, x2d, mu8, *ws, vecs, bd)


def _rwkv_scan_kernel(r_ref, lw_ref, k_ref, v_ref, na_ref, b_ref, g_ref, vec_ref, o_ref, s_ref):
    L, C = r_ref.shape
    N = HEAD_DIM
    PW = 2 * N
    assert PW == V7X_LANES and 2 * L == PW

    @pl.when(pl.program_id(1) == 0)
    def _():
        s_ref[...] = jnp.zeros_like(s_ref)

    def head_a(width, n_rows=L):
        return (lax.broadcasted_iota(jnp.int32, (n_rows, width), 1) & N) == 0

    def split_ab(z):
        m = head_a(z.shape[1])
        return jnp.concatenate([jnp.where(m, z, 0.0), jnp.where(m, 0.0, z)], axis=0)

    def split_ba(z):
        m = head_a(z.shape[1])
        return jnp.concatenate([jnp.where(m, 0.0, z), jnp.where(m, z, 0.0)], axis=0)

    rows = lax.broadcasted_iota(jnp.int32, (L, L), 0)
    cols = lax.broadcasted_iota(jnp.int32, (L, L), 1)
    lw = lw_ref[...]
    tri = jnp.where(rows >= cols, 1.0, 0.0).astype(BF16)
    h1 = lw.astype(BF16)
    r1 = lw - h1.astype(F32)
    h2 = r1.astype(BF16)
    h3 = (r1 - h2.astype(F32)).astype(BF16)
    cs = _dot(tri, h1) + _dot(tri, h2) + _dot(tri, h3)
    cs_last = cs[L - 1:L, :]
    g_incl = jnp.exp(cs)
    g_excl = jnp.exp(cs - lw)
    g_inv = jnp.exp(-cs)
    g_tail = jnp.exp(cs_last - cs)
    g_last = jnp.exp(cs_last)

    r = r_ref[...]
    k = k_ref[...]
    v = v_ref[...]
    b = b_ref[...]
    at = na_ref[...] * g_excl
    rt = r * g_incl
    bt = b * g_inv
    kt = k * g_inv
    bh = b * g_tail
    kh = k * g_tail

    pairs = range(C // PW)
    sls = [slice(j * PW, (j + 1) * PW) for j in pairs]
    t_idx = lax.broadcasted_iota(jnp.int32, (L, PW), 0)
    s_idx = lax.broadcasted_iota(jnp.int32, (L, PW), 1) & (N - 1)
    strict2 = t_idx > s_idx
    incl2 = t_idx >= s_idx
    is_a = head_a(PW)
    is_a2 = head_a(PW, 2 * L)

    lhs = [jnp.concatenate([at[:, sl], rt[:, sl]], axis=0) for sl in sls]
    g_a = [_dot_nt(_bf(jnp.where(is_a2, lhs[j], 0.0)), _bf(jnp.concatenate([bt[:, sls[j]], kt[:, sls[j]]], axis=0)))
           for j in pairs]
    g_b = [_dot_nt(_bf(jnp.where(is_a2, 0.0, lhs[j])), _bf(jnp.concatenate([kt[:, sls[j]], bt[:, sls[j]]], axis=0)))
           for j in pairs]
    pp = [jnp.where(strict2, jnp.where(is_a, g_a[j][:L], g_b[j][:L]), 0.0) for j in pairs]
    m_ak = [jnp.where(strict2, jnp.where(is_a, g_b[j][:L], g_a[j][:L]), 0.0) for j in pairs]
    w_rb = [jnp.where(incl2, jnp.where(is_a, g_a[j][L:], g_b[j][L:]), 0.0) for j in pairs]
    w_rk = [jnp.where(incl2, jnp.where(is_a, g_b[j][L:], g_a[j][L:]), 0.0) for j in pairs]
    mv = [_dot(_bf(m_ak[j]), _bf(split_ba(v[:, sls[j]]))) for j in pairs]
    xs = [jnp.concatenate([at[:, sls[j]], mv[j]], axis=1) for j in pairs]
    levels = int(math.log2(L))
    for lvl in range(levels):
        if lvl < levels - 1:
            z = [_dot(_bf(pp[j]), _bf(split_ab(jnp.concatenate([xs[j], pp[j]], axis=1)))) for j in pairs]
            xs = [xs[j] + z[j][:, :2 * PW] for j in pairs]
            pp = [z[j][:, 2 * PW:] for j in pairs]
        else:
            xs = [xs[j] + _dot(_bf(pp[j]), _bf(split_ab(xs[j]))) for j in pairs]
    zy = [_dot(_bf(jnp.concatenate([w_rb[j], w_rk[j]], axis=1)),
               _bf(jnp.concatenate([split_ab(xs[j]),
                                    split_ba(jnp.concatenate([jnp.zeros((L, PW), F32), v[:, sls[j]]], axis=1))], axis=0)))
          for j in pairs]
    s0 = [s_ref[j] for j in pairs]
    uy = [_dot_nt(_bf(jnp.concatenate([xs[j][:, :PW], rt[:, sls[j]] + zy[j][:, :PW]], axis=0)), _bf(s0[j])) for j in pairs]
    u = [uy[j][:L] + xs[j][:, PW:] for j in pairs]
    y = [uy[j][L:] + zy[j][:, PW:] for j in pairs]
    vr = lax.broadcasted_iota(jnp.int32, (PW, PW), 0)
    kc = lax.broadcasted_iota(jnp.int32, (PW, PW), 1)
    same_head = (vr & N) == (kc & N)
    for j in pairs:
        upd = _dot_tn(_bf(jnp.concatenate([u[j], v[:, sls[j]]], axis=0)),
                      _bf(jnp.concatenate([bh[:, sls[j]], kh[:, sls[j]]], axis=0)))
        s_ref[j] = s0[j] * g_last[:, sls[j]] + jnp.where(same_head, upd, 0.0)

    yy = jnp.concatenate(y, axis=1)
    gw = 2 * PW
    jr = lax.broadcasted_iota(jnp.int32, (gw, gw), 0)
    jc = lax.broadcasted_iota(jnp.int32, (gw, gw), 1)
    ones_bd = jnp.where((jr ^ jc) < N, 1.0, 0.0).astype(BF16)

    def head_sum(z):
        hi, lo = _split2(z)
        return jnp.concatenate([_dot(hi[:, c0:c0 + gw], ones_bd) + _dot(lo[:, c0:c0 + gw], ones_bd)
                                for c0 in range(0, C, gw)], axis=1)

    yd = yy - head_sum(yy) * (1.0 / N)
    yv = head_sum(yd * yd) * (1.0 / N)
    yn = yd * lax.rsqrt(yv + RW_GN_EPS) * vec_ref[0:1, :] + vec_ref[1:2, :]
    bonus = head_sum(r * k * vec_ref[2:3, :]) * v
    o_ref[...] = (yn + bonus) * g_ref[...]


def _rwkv_scan(r, lw, k, v, na, b, g, lnx_g, lnx_b, r_k, batch, seq_len):
    n_tok, c = r.shape
    L = SCAN_CHUNK
    assert seq_len % L == 0
    vecs = jnp.zeros((V7X_SUBLANES, c), F32).at[0].set(lnx_g).at[1].set(lnx_b).at[2].set(r_k.reshape(c))
    chunks = seq_len // L
    blk = pl.BlockSpec((L, c), lambda bi, ci: (bi * chunks + ci, 0))
    return pl.pallas_call(
        _rwkv_scan_kernel,
        grid=(batch, chunks),
        in_specs=[blk] * 7 + [_full(vecs.shape)],
        out_specs=blk,
        out_shape=jax.ShapeDtypeStruct((n_tok, c), F32),
        scratch_shapes=[pltpu.VMEM((c // V7X_LANES, V7X_LANES, V7X_LANES), F32)],
        compiler_params=_params("parallel", "arbitrary"),
        name="rwkv_scan",
    )(r, lw, k, v, na, b, g, vecs)


def _proj_ln_kernel(x_ref, z_ref, w_ref, gb_ref, o_ref):
    m = _dot(_bf(z_ref[...]), w_ref[...])
    o_ref[...] = _layer_norm(DEEPNORM_ALPHA * x_ref[...] + m, gb_ref[0:1, :], gb_ref[1:2, :])


def _proj_ln(x2d, z2d, w, ln_g, ln_b):
    n_tok, c = x2d.shape
    tm = min(2 * ROW_TILE, n_tok)
    gb = jnp.zeros((V7X_SUBLANES, c), F32).at[0].set(ln_g).at[1].set(ln_b)
    row = pl.BlockSpec((tm, c), lambda i: (i, 0))
    return pl.pallas_call(
        _proj_ln_kernel,
        grid=(n_tok // tm,),
        in_specs=[row, row, _full(w.shape), _full(gb.shape)],
        out_specs=row,
        out_shape=jax.ShapeDtypeStruct((n_tok, c), F32),
        compiler_params=_params("parallel"),
        name="proj_ln",
    )(x2d, z2d, _bf(w), gb)


def _router_kernel(x_ref, wt_ref, b_ref, earlier_ref, e_ref, gate_ref, rank_ref, cnt_ref):
    x_hi, x_lo = _split2(x_ref[...])
    w_hi, w_lo = _split2(wt_ref[...])
    logits = _dot_nt(w_hi, x_hi) + _dot_nt(w_hi, x_lo) + _dot_nt(w_lo, x_hi) + b_ref[...]
    n_e = logits.shape[0]
    eid = lax.broadcasted_iota(jnp.int32, logits.shape, 0)
    tops, idxs, hots = [], [], []
    for _ in range(TOP_K):
        m = jnp.max(logits, axis=0, keepdims=True)
        idx = jnp.min(jnp.where(logits == m, eid, n_e), axis=0, keepdims=True)
        tops.append(m)
        idxs.append(idx)
        hots.append(eid == idx)
        logits = jnp.where(eid == idx, -jnp.inf, logits)
    ex = [jnp.exp(t - tops[0]) for t in tops]
    denom = ex[0] + ex[1] + ex[2] + ex[3]
    e_ref[...] = jnp.concatenate(idxs, axis=0)
    gate_ref[...] = jnp.concatenate([e / denom for e in ex], axis=0)
    chosen = sum(jnp.where(hot, 1.0, 0.0) for hot in hots)
    before = _dot(_bf(chosen), earlier_ref[...])
    rank_ref[...] = jnp.concatenate([jnp.sum(jnp.where(hot, before, 0.0), axis=0, keepdims=True) for hot in hots],
                                    axis=0).astype(jnp.int32)
    cnt_ref[...] = jnp.broadcast_to(jnp.sum(chosen, axis=1, keepdims=True), cnt_ref.shape).astype(jnp.int32)


def _router(x2d, w_router, b_router):
    n_tok, c = x2d.shape
    tm = min(MOE_ROUTER_TILE, n_tok)
    wt = w_router.T
    bcol = b_router.reshape(N_EXPERTS, 1)
    tok = jnp.arange(tm, dtype=jnp.int32)
    earlier = (tok[:, None] < tok[None, :]).astype(BF16)
    tile_of = lambda i: (0, i)
    return pl.pallas_call(
        _router_kernel,
        grid=(n_tok // tm,),
        in_specs=[pl.BlockSpec((tm, c), lambda i: (i, 0)), _full(wt.shape), _full(bcol.shape), _full(earlier.shape)],
        out_specs=[pl.BlockSpec((TOP_K, tm), tile_of)] * 3 + [pl.BlockSpec((None, N_EXPERTS, V7X_LANES), lambda i: (i, 0, 0))],
        out_shape=[jax.ShapeDtypeStruct((TOP_K, n_tok), jnp.int32), jax.ShapeDtypeStruct((TOP_K, n_tok), F32),
                   jax.ShapeDtypeStruct((TOP_K, n_tok), jnp.int32),
                   jax.ShapeDtypeStruct((n_tok // tm, N_EXPERTS, V7X_LANES), jnp.int32)],
        compiler_params=_params("parallel"),
        name="moe_router",
    )(x2d, wt, bcol, earlier)


def _expert_kernel(blk_e_ref, n_used_ref, x_ref, wgu_ref, bgu_ref, wd_ref, bd_ref, y_ref, wgu_bf, wd_bf):
    i = pl.program_id(0)
    e = blk_e_ref[i]
    changed = jnp.logical_or(i == 0, e != blk_e_ref[jnp.maximum(i - 1, 0)])

    @pl.when(changed)
    def _():
        wgu_bf[...] = _bf(wgu_ref[...])
        wd_bf[...] = _bf(wd_ref[...])

    @pl.when(i < n_used_ref[0])
    def _():
        f = wd_ref.shape[0]
        gu = _dot(x_ref[...], wgu_bf[...]) + bgu_ref[...]
        glu = jnp.minimum(gu[:, :f], SWIGLU_LIMIT)
        lin = jnp.clip(gu[:, f:], -SWIGLU_LIMIT, SWIGLU_LIMIT)
        act = glu * _sigmoid(SWIGLU_ALPHA * glu) * (lin + 1.0)
        y_ref[...] = _dot(_bf(act), wd_bf[...]) + bd_ref[...]

    @pl.when(i >= n_used_ref[0])
    def _():
        y_ref[...] = jnp.zeros_like(y_ref)


def _experts(x_rows, blk_e, n_used, w_gu, b_gu, w_down, b_down, layer):
    n_rows, c = x_rows.shape
    n_l, n_e, _, f2 = w_gu.shape
    f = f2 // 2
    tb = MOE_TILE
    grid_spec = pltpu.PrefetchScalarGridSpec(
        num_scalar_prefetch=2,
        grid=(n_rows // tb,),
        in_specs=[
            pl.BlockSpec((tb, c), lambda i, be, nu: (i, 0)),
            pl.BlockSpec((None, None, c, f2), lambda i, be, nu: (layer, be[i], 0, 0)),
            pl.BlockSpec((None, None, 1, f2), lambda i, be, nu: (layer, be[i], 0, 0)),
            pl.BlockSpec((None, None, f, c), lambda i, be, nu: (layer, be[i], 0, 0)),
            pl.BlockSpec((None, None, 1, c), lambda i, be, nu: (layer, be[i], 0, 0)),
        ],
        out_specs=pl.BlockSpec((tb, c), lambda i, be, nu: (i, 0)),
        scratch_shapes=[pltpu.VMEM((c, f2), BF16), pltpu.VMEM((f, c), BF16)],
    )
    return pl.pallas_call(
        _expert_kernel,
        grid_spec=grid_spec,
        out_shape=jax.ShapeDtypeStruct((n_rows, c), F32),
        compiler_params=_params("arbitrary"),
        name="moe_experts",
    )(blk_e, n_used, x_rows, w_gu, b_gu.reshape(n_l, n_e, 1, f2), w_down, b_down.reshape(n_l, n_e, 1, c))


def _combine_ln_kernel(x_ref, y_ref, gate_ref, gb_ref, o_ref):
    acc = DEEPNORM_ALPHA * x_ref[...]
    for j in range(TOP_K):
        acc = acc + gate_ref[:, j:j + 1] * y_ref[j]
    o_ref[...] = _layer_norm(acc, gb_ref[0:1, :], gb_ref[1:2, :])


def _combine_ln(x2d, y4, gates, ln_g, ln_b):
    n_tok, c = x2d.shape
    tm = min(ROW_TILE, n_tok)
    gb = jnp.zeros((V7X_SUBLANES, c), F32).at[0].set(ln_g).at[1].set(ln_b)
    return pl.pallas_call(
        _combine_ln_kernel,
        grid=(n_tok // tm,),
        in_specs=[pl.BlockSpec((tm, c), lambda i: (i, 0)), pl.BlockSpec((TOP_K, tm, c), lambda i: (0, i, 0)),
                  pl.BlockSpec((tm, TOP_K), lambda i: (i, 0)), _full(gb.shape)],
        out_specs=pl.BlockSpec((tm, c), lambda i: (i, 0)),
        out_shape=jax.ShapeDtypeStruct((n_tok, c), F32),
        compiler_params=_params("parallel"),
        name="moe_combine_ln",
    )(x2d, y4, gates, gb)


def _moe(x2d, w_router, b_router, w_gu, b_gu, w_down, b_down, ln_g, ln_b, layer):
    n_tok, c = x2d.shape
    top_e, gates, rank, tile_cnt = _router(x2d, w_router, b_router)
    tb = MOE_TILE
    tile_cnt = tile_cnt[:, :, 0]
    n_tiles = tile_cnt.shape[0]
    counts = jnp.sum(tile_cnt, axis=0)
    padded = (counts + tb - 1) // tb * tb
    pad_end = jnp.cumsum(padded)
    pad_start = pad_end - padded
    base = pad_start[None, :] + jnp.cumsum(tile_cnt, axis=0) - tile_cnt
    base_tok = jnp.broadcast_to(base[:, None, :], (n_tiles, n_tok // n_tiles, N_EXPERTS)).reshape(n_tok, N_EXPERTS)
    hot = top_e[:, :, None] == jnp.arange(N_EXPERTS, dtype=jnp.int32)
    dest = jnp.sum(jnp.where(hot, base_tok[None], 0), axis=-1) + rank
    n_rows = -(-(n_tok * TOP_K + N_EXPERTS * (tb - 1)) // tb) * tb
    tok = jnp.broadcast_to(jnp.arange(n_tok, dtype=jnp.int32)[None, :], (TOP_K, n_tok))
    rows_tok = jnp.zeros((n_rows,), jnp.int32).at[dest.reshape(-1)].set(tok.reshape(-1))
    blk_start = jnp.arange(n_rows // tb, dtype=jnp.int32) * tb
    blk_e = jnp.minimum(jnp.sum((pad_end[None, :] <= blk_start[:, None]).astype(jnp.int32), axis=1), N_EXPERTS - 1)
    n_used = (pad_end[-1] // tb).astype(jnp.int32).reshape(1)
    x_rows = _bf(x2d)[rows_tok]
    y_rows = _experts(x_rows, blk_e, n_used, w_gu, b_gu, w_down, b_down, layer)
    y4 = y_rows[dest.reshape(-1)].reshape(TOP_K, n_tok, c)
    return _combine_ln(x2d, y4, gates.T, ln_g, ln_b)


def _rope(z, cos, sin_signed):
    half = HEAD_DIM // 2
    width = z.shape[-1]
    lane = lax.broadcasted_iota(jnp.int32, z.shape, z.ndim - 1)
    swapped = jnp.where(lane % HEAD_DIM < half, pltpu.roll(z, width - half, z.ndim - 1), pltpu.roll(z, half, z.ndim - 1))
    return z * cos + swapped * sin_signed


def _dsa_proj_kernel(x_ref, w_ref, cos_ref, sin_ref, gb_ref, q_ref, k_ref, v_ref, qi_ref, ki_ref, wi_ref):
    c = x_ref.shape[1]
    proj = _dot(_bf(x_ref[...]), w_ref[...])
    cos2 = cos_ref[...]
    sin2 = sin_ref[...]
    q_w = q_ref.shape[1]
    qi_w = qi_ref.shape[1]
    q = proj[:, :q_w]
    q_ref[...] = _bf(_rope(q, jnp.tile(cos2, (1, q_w // V7X_LANES)), jnp.tile(sin2, (1, q_w // V7X_LANES))))
    kv = proj[:, q_w:q_w + V7X_LANES]
    k_ref[...] = _bf(_rope(kv, cos2, sin2)[:, :HEAD_DIM])
    v_ref[...] = _bf(kv[:, HEAD_DIM:])
    o = q_w + V7X_LANES
    qi = proj[:, o:o + qi_w]
    qi_ref[...] = _rope(qi, jnp.tile(cos2, (1, qi_w // V7X_LANES)), jnp.tile(sin2, (1, qi_w // V7X_LANES)))
    o = o + qi_w
    kiw = proj[:, o:o + V7X_LANES]
    in_ki = lax.broadcasted_iota(jnp.int32, kiw.shape, 1) < IDX_DIM
    mu = jnp.sum(kiw, axis=-1, keepdims=True) * (1.0 / IDX_DIM)
    d = jnp.where(in_ki, kiw - mu, 0.0)
    var = jnp.sum(d * d, axis=-1, keepdims=True) * (1.0 / IDX_DIM)
    ki = d * lax.rsqrt(var + LN_EPS) * gb_ref[0:1, :] + gb_ref[1:2, :]
    ki_ref[...] = _rope(ki, cos2, sin2)[:, :IDX_DIM]
    o = o + V7X_LANES
    wi_ref[...] = proj[:, o:o + IDX_HEADS] * IDX_W_SCALE


def _dsa_proj(x2d, seq_len, w_in, idx_g, idx_b, cos2, sin2):
    n_tok, c = x2d.shape
    q_w = N_HEADS * HEAD_DIM
    qi_w = IDX_HEADS * IDX_DIM
    o_ki = q_w + 2 * HEAD_DIM + qi_w
    w_pad = jnp.concatenate([w_in[:, :o_ki + IDX_DIM], jnp.zeros((c, V7X_LANES - IDX_DIM), F32),
                             w_in[:, o_ki + IDX_DIM:], jnp.zeros((c, V7X_LANES - IDX_HEADS), F32)], axis=1)
    tm = min(ROW_TILE, seq_len)
    tiles = seq_len // tm
    gb = jnp.zeros((V7X_SUBLANES, V7X_LANES), F32).at[0, :IDX_DIM].set(idx_g).at[1, :IDX_DIM].set(idx_b)
    row = lambda w: pl.BlockSpec((tm, w), lambda i: (i, 0))
    tab = pl.BlockSpec((tm, V7X_LANES), lambda i: (i % tiles, 0))
    shapes = [(q_w, BF16), (HEAD_DIM, BF16), (HEAD_DIM, BF16), (qi_w, F32), (IDX_DIM, F32), (IDX_HEADS, F32)]
    return pl.pallas_call(
        _dsa_proj_kernel,
        grid=(n_tok // tm,),
        in_specs=[row(c), _full(w_pad.shape), tab, tab, _full(gb.shape)],
        out_specs=[row(w) for w, _ in shapes],
        out_shape=[jax.ShapeDtypeStruct((n_tok, w), dt) for w, dt in shapes],
        compiler_params=_params("parallel"),
        name="dsa_proj",
    )(x2d, _bf(w_pad), cos2, sin2, gb)


def _dsa_attn_kernel(x_ref, q_ref, qi_ref, wi_ref, ki_ref, k_ref, v_ref, wo_ref, gb_ref, o_ref, *, q_block0, n_keys, k_sel):
    qb = q_block0 + pl.program_id(1)
    tq = q_ref.shape[0]
    nk = n_keys

    ki_hi, ki_lo = _split2(ki_ref[0:nk, :])
    ki3 = jnp.concatenate([ki_hi, ki_hi, ki_lo], axis=1)
    qi = qi_ref[...]
    wi = wi_ref[...]
    s = jnp.zeros((tq, nk), F32)
    for h in range(IDX_HEADS):
        q_hi, q_lo = _split2(qi[:, h * IDX_DIM:(h + 1) * IDX_DIM])
        sh = _dot_nt(jnp.concatenate([q_hi, q_lo, q_hi], axis=1), ki3)
        s = s + jnp.maximum(sh, 0.0) * wi[:, h:h + 1]

    key_pos = lax.broadcasted_iota(jnp.int32, (tq, nk), 1)
    q_pos = qb * tq + lax.broadcasted_iota(jnp.int32, (tq, nk), 0)
    causal = key_pos <= q_pos

    bits = pltpu.bitcast(s, jnp.int32)
    keys = jnp.where(causal, bits ^ ((bits >> 31) & jnp.int32(0x7FFFFFFF)), jnp.int32(INT_MIN))

    def bit_step(i, tau):
        cand = tau | jnp.left_shift(jnp.int32(1), 31 - i)
        cnt = jnp.sum(jnp.where(keys >= (cand ^ jnp.int32(INT_MIN)), 1.0, 0.0), axis=1, keepdims=True)
        return jnp.where(cnt >= k_sel, cand, tau)

    tau = lax.fori_loop(0, 32, bit_step, jnp.zeros((tq, 1), jnp.int32)) ^ jnp.int32(INT_MIN)
    above = keys > tau
    tied = keys == tau
    room = k_sel - jnp.sum(jnp.where(above, 1.0, 0.0), axis=1, keepdims=True)
    tied_bf = jnp.where(tied, 1.0, 0.0).astype(BF16)
    cr = lax.broadcasted_iota(jnp.int32, (V7X_LANES, V7X_LANES), 0)
    cc = lax.broadcasted_iota(jnp.int32, (V7X_LANES, V7X_LANES), 1)
    earlier_bf = jnp.where(cr < cc, 1.0, 0.0).astype(BF16)
    carry = jnp.zeros((tq, 1), F32)
    before = []
    for ch in range(nk // V7X_LANES):
        t = tied_bf[:, ch * V7X_LANES:(ch + 1) * V7X_LANES]
        before.append(_dot(t, earlier_bf) + carry)
        carry = carry + jnp.sum(t.astype(F32), axis=1, keepdims=True)
    before = jnp.concatenate(before, axis=1)
    sel = jnp.logical_and(jnp.logical_or(above, jnp.logical_and(tied, before < room)), causal)
    bias = jnp.where(sel, 0.0, -1e30)

    k = k_ref[0:nk, :]
    v = v_ref[0:nk, :]
    q = q_ref[...] * (HEAD_DIM ** -0.5)
    n_heads = q.shape[1] // HEAD_DIM
    outs = []
    for g0 in range(0, n_heads, DSA_HEAD_GROUP):
        hs = range(g0, g0 + DSA_HEAD_GROUP)
        logits = _dot_nt(jnp.concatenate([q[:, h * HEAD_DIM:(h + 1) * HEAD_DIM] for h in hs], axis=0), k)
        ps, inv = [], []
        for i in range(DSA_HEAD_GROUP):
            lg = logits[i * tq:(i + 1) * tq, :] + bias
            p = jnp.exp(lg - jnp.max(lg, axis=1, keepdims=True))
            inv.append(1.0 / jnp.sum(p, axis=1, keepdims=True))
            ps.append(_bf(p))
        o = _dot(jnp.concatenate(ps, axis=0), v)
        outs += [o[i * tq:(i + 1) * tq, :] * inv[i] for i in range(DSA_HEAD_GROUP)]
    att = jnp.concatenate(outs, axis=1)
    m_out = _dot(_bf(att), wo_ref[...])
    o_ref[...] = _layer_norm(DEEPNORM_ALPHA * x_ref[...] + m_out, gb_ref[0:1, :], gb_ref[1:2, :])


def _dsa_attn(x2d, batch, seq_len, q, k, v, qi, ki, wi, w_o, ln_g, ln_b):
    n_tok, c = x2d.shape
    tq = Q_TILE
    k_sel = min(TOPK_MAX, seq_len // 4)
    kg = min(KEY_GROUP, seq_len)
    assert seq_len % kg == 0 and kg % tq == 0
    qb_per_group = kg // tq
    qb_per_seq = seq_len // tq
    ki3 = ki.reshape(batch, seq_len, IDX_DIM)
    k3 = k.reshape(batch, seq_len, HEAD_DIM)
    v3 = v.reshape(batch, seq_len, HEAD_DIM)
    gb = jnp.zeros((V7X_SUBLANES, c), F32).at[0].set(ln_g).at[1].set(ln_b)
    wo = _bf(w_o)
    outs = []
    for grp in range(seq_len // kg):
        q0 = grp * qb_per_group
        n_keys = (grp + 1) * kg
        row = lambda w: pl.BlockSpec((tq, w), lambda bi, j: (bi * qb_per_seq + q0 + j, 0))
        per_batch = lambda w: pl.BlockSpec((None, seq_len, w), lambda bi, j: (bi, 0, 0))
        out = pl.pallas_call(
            functools.partial(_dsa_attn_kernel, q_block0=q0, n_keys=n_keys, k_sel=k_sel),
            grid=(batch, qb_per_group),
            in_specs=[row(c), row(q.shape[1]), row(qi.shape[1]), row(wi.shape[1]),
                      per_batch(IDX_DIM), per_batch(HEAD_DIM), per_batch(HEAD_DIM),
                      _full(wo.shape), _full(gb.shape)],
            out_specs=pl.BlockSpec((None, tq, c), lambda bi, j: (bi, j, 0)),
            out_shape=jax.ShapeDtypeStruct((batch, kg, c), F32),
            compiler_params=_params("parallel", "parallel"),
            name=f"dsa_attn_{n_keys}",
        )(x2d, q, qi, wi, ki3, k3, v3, wo, gb)
        outs.append(out)
    return jnp.concatenate(outs, axis=1).reshape(n_tok, c)


def _rope_tables(seq_len):
    pos = jnp.arange(seq_len, dtype=F32)
    inv = 1.0 / (ROPE_THETA ** (jnp.arange(0, HEAD_DIM, 2, dtype=F32) / HEAD_DIM))
    ang = pos[:, None] * inv[None, :]
    cos, sin = jnp.cos(ang), jnp.sin(ang)
    cos2 = jnp.tile(jnp.concatenate([cos, cos], axis=1), (1, V7X_LANES // HEAD_DIM))
    sin2 = jnp.tile(jnp.concatenate([-sin, sin], axis=1), (1, V7X_LANES // HEAD_DIM))
    return cos2, sin2


def kernel(x, rw_mu, rw_w_rkv, rw_w0, rw_w1, rw_w2, rw_a0, rw_a1, rw_a2, rw_g1, rw_g2, rw_k_k, rw_k_a, rw_r_k, rw_lnx_g, rw_lnx_b, rw_w_o, ds_w_in, ds_idx_ln_g, ds_idx_ln_b, ds_w_o, ln_mix_g, ln_mix_b, moe_w_router, moe_b_router, moe_w_gu, moe_b_gu, moe_w_down, moe_b_down, ln_ffn_g, ln_ffn_b):
    batch, seq_len, c = x.shape
    assert c == D_MODEL
    h = x.reshape(batch * seq_len, c)
    cos2, sin2 = _rope_tables(seq_len)
    head_of = jnp.arange(2 * V7X_LANES, dtype=jnp.int32) // HEAD_DIM
    bd = (head_of[:, None] == head_of[None, :]).astype(BF16)
    n_mixers = 2
    for i in range(DEPTH):
        j = i // n_mixers
        if i % n_mixers == 0:
            r, lw, k, v, na, b, g = _rwkv_pre(h, seq_len, rw_mu[j], rw_w_rkv[j], rw_w0[j], rw_w1[j], rw_w2[j],
                                              rw_a0[j], rw_a1[j], rw_a2[j], rw_g1[j], rw_g2[j], rw_k_k[j], rw_k_a[j], bd)
            z = _rwkv_scan(r, lw, k, v, na, b, g, rw_lnx_g[j], rw_lnx_b[j], rw_r_k[j], batch, seq_len)
            h = _proj_ln(h, z, rw_w_o[j], ln_mix_g[i], ln_mix_b[i])
        else:
            q, k, v, qi, ki, wi = _dsa_proj(h, seq_len, ds_w_in[j], ds_idx_ln_g[j], ds_idx_ln_b[j], cos2, sin2)
            h = _dsa_attn(h, batch, seq_len, q, k, v, qi, ki, wi, ds_w_o[j], ln_mix_g[i], ln_mix_b[i])
        h = _moe(h, moe_w_router[i], moe_b_router[i], moe_w_gu, moe_b_gu, moe_w_down, moe_b_down,
                 ln_ffn_g[i], ln_ffn_b[i], i)
    return h.reshape(batch, seq_len, c)
```

```python
import functools
import math

import jax
import jax.numpy as jnp
from jax import lax
from jax.experimental import pallas as pl
from jax.experimental.pallas import tpu as pltpu

D_MODEL = 1024
HEAD_DIM = 64
N_HEADS = D_MODEL // HEAD_DIM
RW_GN_EPS = 64e-5
IDX_HEADS = 8
IDX_DIM = 64
TOPK_MAX = 256
ROPE_THETA = 10000.0
N_EXPERTS = 32
TOP_K = 4
SWIGLU_LIMIT = 7.0
SWIGLU_ALPHA = 1.702
LN_EPS = 1e-5
DEPTH = 2
DEEPNORM_ALPHA = (2.0 * DEPTH) ** 0.25
IDX_W_SCALE = IDX_HEADS ** -0.5 * IDX_DIM ** -0.5

V7X_LANES = 128
V7X_SUBLANES = 8
V7X_VMEM_LIMIT_BYTES = 56 * 1024 * 1024

SCAN_CHUNK = 64
ROW_TILE = 256
Q_TILE = 128
KEY_GROUP = 512
MOE_TILE = 512
MOE_ROUTER_TILE = 1024
DSA_HEAD_GROUP = 4

F32 = jnp.float32
BF16 = jnp.bfloat16
INT_MIN = -2147483648


def _bf(x):
    return x.astype(BF16)


def _split2(x):
    hi = x.astype(BF16)
    lo = (x - hi.astype(F32)).astype(BF16)
    return hi, lo


def _dot(a, b):
    return jnp.dot(a, b, preferred_element_type=F32)


def _dot_nt(a, b):
    return lax.dot_general(a, b, (((1,), (1,)), ((), ())), preferred_element_type=F32)


def _dot_tn(a, b):
    return lax.dot_general(a, b, (((0,), (0,)), ((), ())), preferred_element_type=F32)


def _layer_norm(h, g, b):
    mu = jnp.mean(h, axis=-1, keepdims=True)
    d = h - mu
    var = jnp.mean(d * d, axis=-1, keepdims=True)
    return d * lax.rsqrt(var + LN_EPS) * g + b


def _sigmoid(z):
    return 1.0 / (1.0 + jnp.exp(-z))


def _params(*sem):
    return pltpu.CompilerParams(dimension_semantics=sem, vmem_limit_bytes=V7X_VMEM_LIMIT_BYTES)


def _full(shape):
    n = len(shape)
    return pl.BlockSpec(shape, lambda *_: (0,) * n)


def _rwkv_pre_kernel(x_ref, xp_ref, mu_ref, wrkv_ref, w1_ref, w2_ref, a1_ref, a2_ref, g1_ref, g2_ref,
                     vec_ref, bd_ref, r_ref, lw_ref, k_ref, v_ref, na_ref, b_ref, g_ref, *, tiles_per_seq):
    i = pl.program_id(0)
    x = x_ref[...]
    prev = xp_ref[V7X_SUBLANES - 1:V7X_SUBLANES, :]
    prev = jnp.where(i % tiles_per_seq == 0, 0.0, prev)
    rows = lax.broadcasted_iota(jnp.int32, x.shape, 0)
    xprev = jnp.where(rows == 0, prev, pltpu.roll(x, 1, 0))
    xx = xprev - x

    def mix(p):
        return x + xx * mu_ref[p:p + 1, :]

    w0 = vec_ref[0:1, :]
    a0 = vec_ref[1:2, :]
    k_k = vec_ref[2:3, :]
    k_a = vec_ref[3:4, :]

    r = _dot(_bf(mix(0)), wrkv_ref[0])
    k = _dot(_bf(mix(1)), wrkv_ref[1])
    v = _dot(_bf(mix(2)), wrkv_ref[2])

    zw = w0 + _dot(_bf(jnp.tanh(_dot(_bf(mix(3)), w1_ref[...]))), w2_ref[...])
    nz = -zw
    softplus = jnp.maximum(nz, 0.0) + jnp.log1p(jnp.exp(-jnp.abs(nz)))
    w_log = -softplus - 0.5
    lw = -jnp.exp(w_log)

    a = _sigmoid(a0 + _dot(_bf(_dot(_bf(mix(4)), a1_ref[...])), a2_ref[...]))
    g = _dot(_bf(_sigmoid(_dot(_bf(mix(5)), g1_ref[...]))), g2_ref[...])

    kk = k * k_k
    sq_hi, sq_lo = _split2(kk * kk)
    gw = bd_ref.shape[0]
    ss = jnp.concatenate([_dot(sq_hi[:, c0:c0 + gw], bd_ref[...]) + _dot(sq_lo[:, c0:c0 + gw], bd_ref[...])
                          for c0 in range(0, kk.shape[1], gw)], axis=1)
    kk = kk / jnp.maximum(jnp.sqrt(ss), 1e-12)
    k = k * (1.0 + (a - 1.0) * k_a)

    r_ref[...] = r
    lw_ref[...] = lw
    k_ref[...] = k
    v_ref[...] = v
    na_ref[...] = -kk
    b_ref[...] = kk * a
    g_ref[...] = g


def _rwkv_pre(x2d, seq_len, mu, w_rkv, w0, w1, w2, a0, a1, a2, g1, g2, k_k, k_a, bd):
    n_tok, c = x2d.shape
    tm = min(ROW_TILE, seq_len)
    assert seq_len % tm == 0 and tm % V7X_SUBLANES == 0
    vecs = jnp.zeros((V7X_SUBLANES, c), F32).at[0].set(w0).at[1].set(a0).at[2].set(k_k).at[3].set(k_a)
    mu8 = jnp.zeros((V7X_SUBLANES, c), F32).at[:6].set(mu)
    row = pl.BlockSpec((tm, c), lambda i: (i, 0))
    sub_per_tile = tm // V7X_SUBLANES
    prev = pl.BlockSpec((V7X_SUBLANES, c), lambda i: (jnp.maximum(i * sub_per_tile - 1, 0), 0))
    ws = [_bf(w_rkv), _bf(w1), _bf(w2), _bf(a1), _bf(a2), _bf(g1), _bf(g2)]
    out = jax.ShapeDtypeStruct((n_tok, c), F32)
    return pl.pallas_call(
        functools.partial(_rwkv_pre_kernel, tiles_per_seq=seq_len // tm),
        grid=(n_tok // tm,),
        in_specs=[row, prev, _full(mu8.shape)] + [_full(w.shape) for w in ws] + [_full(vecs.shape), _full(bd.shape)],
        out_specs=[row] * 7,
        out_shape=[out] * 7,
        compiler_params=_params("parallel"),
        name="rwkv_pre",
    )(x2d, x2d, mu8, *ws, vecs, bd)


def _rwkv_scan_kernel(r_ref, lw_ref, k_ref, v_ref, na_ref, b_ref, g_ref, vec_ref, o_ref, s_ref):
    L, C = r_ref.shape
    N = HEAD_DIM
    PW = 2 * N
    assert PW == V7X_LANES and 2 * L == PW

    @pl.when(pl.program_id(1) == 0)
    def _():
        s_ref[...] = jnp.zeros_like(s_ref)

    def head_a(width, n_rows=L):
        return (lax.broadcasted_iota(jnp.int32, (n_rows, width), 1) & N) == 0

    def split_ab(z):
        m = head_a(z.shape[1])
        return jnp.concatenate([jnp.where(m, z, 0.0), jnp.where(m, 0.0, z)], axis=0)

    def split_ba(z):
        m = head_a(z.shape[1])
        return jnp.concatenate([jnp.where(m, 0.0, z), jnp.where(m, z, 0.0)], axis=0)

    rows = lax.broadcasted_iota(jnp.int32, (L, L), 0)
    cols = lax.broadcasted_iota(jnp.int32, (L, L), 1)
    lw = lw_ref[...]
    tri = jnp.where(rows >= cols, 1.0, 0.0).astype(BF16)
    h1 = lw.astype(BF16)
    r1 = lw - h1.astype(F32)
    h2 = r1.astype(BF16)
    h3 = (r1 - h2.astype(F32)).astype(BF16)
    cs = _dot(tri, h1) + _dot(tri, h2) + _dot(tri, h3)
    cs_last = cs[L - 1:L, :]
    g_incl = jnp.exp(cs)
    g_excl = jnp.exp(cs - lw)
    g_inv = jnp.exp(-cs)
    g_tail = jnp.exp(cs_last - cs)
    g_last = jnp.exp(cs_last)

    r = r_ref[...]
    k = k_ref[...]
    v = v_ref[...]
    b = b_ref[...]
    at = na_ref[...] * g_excl
    rt = r * g_incl
    bt = b * g_inv
    kt = k * g_inv
    bh = b * g_tail
    kh = k * g_tail

    pairs = range(C // PW)
    sls = [slice(j * PW, (j + 1) * PW) for j in pairs]
    t_idx = lax.broadcasted_iota(jnp.int32, (L, PW), 0)
    s_idx = lax.broadcasted_iota(jnp.int32, (L, PW), 1) & (N - 1)
    strict2 = t_idx > s_idx
    incl2 = t_idx >= s_idx
    is_a = head_a(PW)
    is_a2 = head_a(PW, 2 * L)

    lhs = [jnp.concatenate([at[:, sl], rt[:, sl]], axis=0) for sl in sls]
    g_a = [_dot_nt(_bf(jnp.where(is_a2, lhs[j], 0.0)), _bf(jnp.concatenate([bt[:, sls[j]], kt[:, sls[j]]], axis=0)))
           for j in pairs]
    g_b = [_dot_nt(_bf(jnp.where(is_a2, 0.0, lhs[j])), _bf(jnp.concatenate([kt[:, sls[j]], bt[:, sls[j]]], axis=0)))
           for j in pairs]
    pp = [jnp.where(strict2, jnp.where(is_a, g_a[j][:L], g_b[j][:L]), 0.0) for j in pairs]
    m_ak = [jnp.where(strict2, jnp.where(is_a, g_b[j][:L], g_a[j][:L]), 0.0) for j in pairs]
    w_rb = [jnp.where(incl2, jnp.where(is_a, g_a[j][L:], g_b[j][L:]), 0.0) for j in pairs]
    w_rk = [jnp.where(incl2, jnp.where(is_a, g_b[j][L:], g_a[j][L:]), 0.0) for j in pairs]
    mv = [_dot(_bf(m_ak[j]), _bf(split_ba(v[:, sls[j]]))) for j in pairs]
    xs = [jnp.concatenate([at[:, sls[j]], mv[j]], axis=1) for j in pairs]
    levels = int(math.log2(L))
    for lvl in range(levels):
        if lvl < levels - 1:
            z = [_dot(_bf(pp[j]), _bf(split_ab(jnp.concatenate([xs[j], pp[j]], axis=1)))) for j in pairs]
            xs = [xs[j] + z[j][:, :2 * PW] for j in pairs]
            pp = [z[j][:, 2 * PW:] for j in pairs]
        else:
            xs = [xs[j] + _dot(_bf(pp[j]), _bf(split_ab(xs[j]))) for j in pairs]
    zy = [_dot(_bf(jnp.concatenate([w_rb[j], w_rk[j]], axis=1)),
               _bf(jnp.concatenate([split_ab(xs[j]),
                                    split_ba(jnp.concatenate([jnp.zeros((L, PW), F32), v[:, sls[j]]], axis=1))], axis=0)))
          for j in pairs]
    s0 = [s_ref[j] for j in pairs]
    uy = [_dot_nt(_bf(jnp.concatenate([xs[j][:, :PW], rt[:, sls[j]] + zy[j][:, :PW]], axis=0)), _bf(s0[j])) for j in pairs]
    u = [uy[j][:L] + xs[j][:, PW:] for j in pairs]
    y = [uy[j][L:] + zy[j][:, PW:] for j in pairs]
    vr = lax.broadcasted_iota(jnp.int32, (PW, PW), 0)
    kc = lax.broadcasted_iota(jnp.int32, (PW, PW), 1)
    same_head = (vr & N) == (kc & N)
    for j in pairs:
        upd = _dot_tn(_bf(jnp.concatenate([u[j], v[:, sls[j]]], axis=0)),
                      _bf(jnp.concatenate([bh[:, sls[j]], kh[:, sls[j]]], axis=0)))
        s_ref[j] = s0[j] * g_last[:, sls[j]] + jnp.where(same_head, upd, 0.0)

    yy = jnp.concatenate(y, axis=1)
    gw = 2 * PW
    jr = lax.broadcasted_iota(jnp.int32, (gw, gw), 0)
    jc = lax.broadcasted_iota(jnp.int32, (gw, gw), 1)
    ones_bd = jnp.where((jr ^ jc) < N, 1.0, 0.0).astype(BF16)

    def head_sum(z):
        hi, lo = _split2(z)
        return jnp.concatenate([_dot(hi[:, c0:c0 + gw], ones_bd) + _dot(lo[:, c0:c0 + gw], ones_bd)
                                for c0 in range(0, C, gw)], axis=1)

    yd = yy - head_sum(yy) * (1.0 / N)
    yv = head_sum(yd * yd) * (1.0 / N)
    yn = yd * lax.rsqrt(yv + RW_GN_EPS) * vec_ref[0:1, :] + vec_ref[1:2, :]
    bonus = head_sum(r * k * vec_ref[2:3, :]) * v
    o_ref[...] = (yn + bonus) * g_ref[...]


def _rwkv_scan(r, lw, k, v, na, b, g, lnx_g, lnx_b, r_k, batch, seq_len):
    n_tok, c = r.shape
    L = SCAN_CHUNK
    assert seq_len % L == 0
    vecs = jnp.zeros((V7X_SUBLANES, c), F32).at[0].set(lnx_g).at[1].set(lnx_b).at[2].set(r_k.reshape(c))
    chunks = seq_len // L
    blk = pl.BlockSpec((L, c), lambda bi, ci: (bi * chunks + ci, 0))
    return pl.pallas_call(
        _rwkv_scan_kernel,
        grid=(batch, chunks),
        in_specs=[blk] * 7 + [_full(vecs.shape)],
        out_specs=blk,
        out_shape=jax.ShapeDtypeStruct((n_tok, c), F32),
        scratch_shapes=[pltpu.VMEM((c // V7X_LANES, V7X_LANES, V7X_LANES), F32)],
        compiler_params=_params("parallel", "arbitrary"),
        name="rwkv_scan",
    )(r, lw, k, v, na, b, g, vecs)


def _proj_ln_kernel(x_ref, z_ref, w_ref, gb_ref, o_ref):
    m = _dot(_bf(z_ref[...]), w_ref[...])
    o_ref[...] = _layer_norm(DEEPNORM_ALPHA * x_ref[...] + m, gb_ref[0:1, :], gb_ref[1:2, :])


def _proj_ln(x2d, z2d, w, ln_g, ln_b):
    n_tok, c = x2d.shape
    tm = min(2 * ROW_TILE, n_tok)
    gb = jnp.zeros((V7X_SUBLANES, c), F32).at[0].set(ln_g).at[1].set(ln_b)
    row = pl.BlockSpec((tm, c), lambda i: (i, 0))
    return pl.pallas_call(
        _proj_ln_kernel,
        grid=(n_tok // tm,),
        in_specs=[row, row, _full(w.shape), _full(gb.shape)],
        out_specs=row,
        out_shape=jax.ShapeDtypeStruct((n_tok, c), F32),
        compiler_params=_params("parallel"),
        name="proj_ln",
    )(x2d, z2d, _bf(w), gb)


def _router_kernel(x_ref, wt_ref, b_ref, earlier_ref, e_ref, gate_ref, rank_ref, cnt_ref):
    x_hi, x_lo = _split2(x_ref[...])
    w_hi, w_lo = _split2(wt_ref[...])
    logits = _dot_nt(w_hi, x_hi) + _dot_nt(w_hi, x_lo) + _dot_nt(w_lo, x_hi) + b_ref[...]
    n_e = logits.shape[0]
    eid = lax.broadcasted_iota(jnp.int32, logits.shape, 0)
    tops, idxs, hots = [], [], []
    for _ in range(TOP_K):
        m = jnp.max(logits, axis=0, keepdims=True)
        idx = jnp.min(jnp.where(logits == m, eid, n_e), axis=0, keepdims=True)
        tops.append(m)
        idxs.append(idx)
        hots.append(eid == idx)
        logits = jnp.where(eid == idx, -jnp.inf, logits)
    ex = [jnp.exp(t - tops[0]) for t in tops]
    denom = ex[0] + ex[1] + ex[2] + ex[3]
    e_ref[...] = jnp.concatenate(idxs, axis=0)
    gate_ref[...] = jnp.concatenate([e / denom for e in ex], axis=0)
    chosen = sum(jnp.where(hot, 1.0, 0.0) for hot in hots)
    before = _dot(_bf(chosen), earlier_ref[...])
    rank_ref[...] = jnp.concatenate([jnp.sum(jnp.where(hot, before, 0.0), axis=0, keepdims=True) for hot in hots],
                                    axis=0).astype(jnp.int32)
    cnt_ref[...] = jnp.broadcast_to(jnp.sum(chosen, axis=1, keepdims=True), cnt_ref.shape).astype(jnp.int32)


def _router(x2d, w_router, b_router):
    n_tok, c = x2d.shape
    tm = min(MOE_ROUTER_TILE, n_tok)
    wt = w_router.T
    bcol = b_router.reshape(N_EXPERTS, 1)
    tok = jnp.arange(tm, dtype=jnp.int32)
    earlier = (tok[:, None] < tok[None, :]).astype(BF16)
    tile_of = lambda i: (0, i)
    return pl.pallas_call(
        _router_kernel,
        grid=(n_tok // tm,),
        in_specs=[pl.BlockSpec((tm, c), lambda i: (i, 0)), _full(wt.shape), _full(bcol.shape), _full(earlier.shape)],
        out_specs=[pl.BlockSpec((TOP_K, tm), tile_of)] * 3 + [pl.BlockSpec((None, N_EXPERTS, V7X_LANES), lambda i: (i, 0, 0))],
        out_shape=[jax.ShapeDtypeStruct((TOP_K, n_tok), jnp.int32), jax.ShapeDtypeStruct((TOP_K, n_tok), F32),
                   jax.ShapeDtypeStruct((TOP_K, n_tok), jnp.int32),
                   jax.ShapeDtypeStruct((n_tok // tm, N_EXPERTS, V7X_LANES), jnp.int32)],
        compiler_params=_params("parallel"),
        name="moe_router",
    )(x2d, wt, bcol, earlier)


def _row_copy(src, src_row, dst, dst_row, sem):
    return pltpu.make_async_copy(src.at[pl.ds(src_row, 1)], dst.at[pl.ds(dst_row, 1)], sem)


def _dispatch_kernel(pad_lo_ref, pad_hi_ref, dest_ref, x_hbm, rows_hbm, zero_ref, sem):
    i = pl.program_id(0)
    tm = dest_ref.shape[1]

    @pl.when(i == 0)
    def _():
        zero_ref[...] = jnp.zeros_like(zero_ref)
        for e in range(N_EXPERTS):
            lo, hi = pad_lo_ref[e], pad_hi_ref[e]

            def fill(r, carry):
                _row_copy(zero_ref, 0, rows_hbm, r, sem).start()
                return carry

            def fill_done(r, carry):
                _row_copy(zero_ref, 0, rows_hbm, r, sem).wait()
                return carry

            lax.fori_loop(lo, hi, fill, 0)
            lax.fori_loop(lo, hi, fill_done, 0)

    def issue(n, carry):
        for j in range(TOP_K):
            _row_copy(x_hbm, i * tm + n, rows_hbm, dest_ref[j, n], sem).start()
        return carry

    def done(n, carry):
        for j in range(TOP_K):
            _row_copy(x_hbm, i * tm + n, rows_hbm, dest_ref[j, n], sem).wait()
        return carry

    lax.fori_loop(0, tm, issue, 0)
    lax.fori_loop(0, tm, done, 0)


def _dispatch(x2d, dest, pad_lo, pad_hi, n_rows):
    n_tok, c = x2d.shape
    tm = min(MOE_ROUTER_TILE, n_tok)
    grid_spec = pltpu.PrefetchScalarGridSpec(
        num_scalar_prefetch=2,
        grid=(n_tok // tm,),
        in_specs=[pl.BlockSpec((TOP_K, tm), lambda i, lo, hi: (0, i), memory_space=pltpu.SMEM),
                  pl.BlockSpec(memory_space=pl.ANY)],
        out_specs=pl.BlockSpec(memory_space=pl.ANY),
        scratch_shapes=[pltpu.VMEM((V7X_SUBLANES, c), F32), pltpu.SemaphoreType.DMA(())],
    )
    return pl.pallas_call(
        _dispatch_kernel,
        grid_spec=grid_spec,
        out_shape=jax.ShapeDtypeStruct((n_rows, c), F32),
        compiler_params=_params("arbitrary"),
        name="moe_dispatch",
    )(pad_lo, pad_hi, dest, x2d)


def _expert_kernel(blk_e_ref, n_used_ref, x_ref, wgu_ref, bgu_ref, wd_ref, bd_ref, y_ref, wgu_bf, wd_bf):
    i = pl.program_id(0)
    e = blk_e_ref[i]
    changed = jnp.logical_or(i == 0, e != blk_e_ref[jnp.maximum(i - 1, 0)])

    @pl.when(changed)
    def _():
        wgu_bf[...] = _bf(wgu_ref[...])
        wd_bf[...] = _bf(wd_ref[...])

    @pl.when(i < n_used_ref[0])
    def _():
        f = wd_ref.shape[0]
        gu = _dot(_bf(x_ref[...]), wgu_bf[...]) + bgu_ref[...]
        glu = jnp.minimum(gu[:, :f], SWIGLU_LIMIT)
        lin = jnp.clip(gu[:, f:], -SWIGLU_LIMIT, SWIGLU_LIMIT)
        act = glu * _sigmoid(SWIGLU_ALPHA * glu) * (lin + 1.0)
        y_ref[...] = _dot(_bf(act), wd_bf[...]) + bd_ref[...]

    @pl.when(i >= n_used_ref[0])
    def _():
        y_ref[...] = jnp.zeros_like(y_ref)


def _experts(x_rows, blk_e, n_used, w_gu, b_gu, w_down, b_down, layer):
    n_rows, c = x_rows.shape
    n_l, n_e, _, f2 = w_gu.shape
    f = f2 // 2
    tb = MOE_TILE
    grid_spec = pltpu.PrefetchScalarGridSpec(
        num_scalar_prefetch=2,
        grid=(n_rows // tb,),
        in_specs=[
            pl.BlockSpec((tb, c), lambda i, be, nu: (jnp.minimum(i, nu[0] - 1), 0)),
            pl.BlockSpec((None, None, c, f2), lambda i, be, nu: (layer, be[i], 0, 0)),
            pl.BlockSpec((None, None, 1, f2), lambda i, be, nu: (layer, be[i], 0, 0)),
            pl.BlockSpec((None, None, f, c), lambda i, be, nu: (layer, be[i], 0, 0)),
            pl.BlockSpec((None, None, 1, c), lambda i, be, nu: (layer, be[i], 0, 0)),
        ],
        out_specs=pl.BlockSpec((tb, c), lambda i, be, nu: (i, 0)),
        scratch_shapes=[pltpu.VMEM((c, f2), BF16), pltpu.VMEM((f, c), BF16)],
    )
    return pl.pallas_call(
        _expert_kernel,
        grid_spec=grid_spec,
        out_shape=jax.ShapeDtypeStruct((n_rows, c), F32),
        compiler_params=_params("arbitrary"),
        name="moe_experts",
    )(blk_e, n_used, x_rows, w_gu, b_gu.reshape(n_l, n_e, 1, f2), w_down, b_down.reshape(n_l, n_e, 1, c))


def _combine_ln_kernel(x_ref, y_ref, gate_ref, gb_ref, o_ref):
    acc = DEEPNORM_ALPHA * x_ref[...]
    for j in range(TOP_K):
        acc = acc + gate_ref[:, j:j + 1] * y_ref[j]
    o_ref[...] = _layer_norm(acc, gb_ref[0:1, :], gb_ref[1:2, :])


def _combine_ln(x2d, y4, gates, ln_g, ln_b):
    n_tok, c = x2d.shape
    tm = min(ROW_TILE, n_tok)
    gb = jnp.zeros((V7X_SUBLANES, c), F32).at[0].set(ln_g).at[1].set(ln_b)
    return pl.pallas_call(
        _combine_ln_kernel,
        grid=(n_tok // tm,),
        in_specs=[pl.BlockSpec((tm, c), lambda i: (i, 0)), pl.BlockSpec((TOP_K, tm, c), lambda i: (0, i, 0)),
                  pl.BlockSpec((tm, TOP_K), lambda i: (i, 0)), _full(gb.shape)],
        out_specs=pl.BlockSpec((tm, c), lambda i: (i, 0)),
        out_shape=jax.ShapeDtypeStruct((n_tok, c), F32),
        compiler_params=_params("parallel"),
        name="moe_combine_ln",
    )(x2d, y4, gates, gb)


def _moe(x2d, w_router, b_router, w_gu, b_gu, w_down, b_down, ln_g, ln_b, layer):
    n_tok, c = x2d.shape
    top_e, gates, rank, tile_cnt = _router(x2d, w_router, b_router)
    tb = MOE_TILE
    tile_cnt = tile_cnt[:, :, 0]
    n_tiles = tile_cnt.shape[0]
    counts = jnp.sum(tile_cnt, axis=0)
    padded = (counts + tb - 1) // tb * tb
    pad_end = jnp.cumsum(padded)
    pad_start = pad_end - padded
    base = pad_start[None, :] + jnp.cumsum(tile_cnt, axis=0) - tile_cnt
    base_tok = jnp.broadcast_to(base[:, None, :], (n_tiles, n_tok // n_tiles, N_EXPERTS)).reshape(n_tok, N_EXPERTS)
    hot = top_e[:, :, None] == jnp.arange(N_EXPERTS, dtype=jnp.int32)
    dest = jnp.sum(jnp.where(hot, base_tok[None], 0), axis=-1) + rank
    n_rows = -(-(n_tok * TOP_K + N_EXPERTS * (tb - 1)) // tb) * tb
    blk_start = jnp.arange(n_rows // tb, dtype=jnp.int32) * tb
    blk_e = jnp.minimum(jnp.sum((pad_end[None, :] <= blk_start[:, None]).astype(jnp.int32), axis=1), N_EXPERTS - 1)
    n_used = (pad_end[-1] // tb).astype(jnp.int32).reshape(1)
    x_rows = _dispatch(x2d, dest, pad_start + counts, pad_end, n_rows)
    y_rows = _experts(x_rows, blk_e, n_used, w_gu, b_gu, w_down, b_down, layer)
    y4 = y_rows[dest.reshape(-1)].reshape(TOP_K, n_tok, c)
    return _combine_ln(x2d, y4, gates.T, ln_g, ln_b)


def _rope(z, cos, sin_signed):
    half = HEAD_DIM // 2
    width = z.shape[-1]
    lane = lax.broadcasted_iota(jnp.int32, z.shape, z.ndim - 1)
    swapped = jnp.where(lane % HEAD_DIM < half, pltpu.roll(z, width - half, z.ndim - 1), pltpu.roll(z, half, z.ndim - 1))
    return z * cos + swapped * sin_signed


def _dsa_proj_kernel(x_ref, w_ref, cos_ref, sin_ref, gb_ref, q_ref, k_ref, v_ref, qi_ref, ki_ref, wi_ref):
    c = x_ref.shape[1]
    proj = _dot(_bf(x_ref[...]), w_ref[...])
    cos2 = cos_ref[...]
    sin2 = sin_ref[...]
    q_w = q_ref.shape[1]
    qi_w = qi_ref.shape[1]
    q = proj[:, :q_w]
    q_ref[...] = _bf(_rope(q, jnp.tile(cos2, (1, q_w // V7X_LANES)), jnp.tile(sin2, (1, q_w // V7X_LANES))))
    kv = proj[:, q_w:q_w + V7X_LANES]
    k_ref[...] = _bf(_rope(kv, cos2, sin2)[:, :HEAD_DIM])
    v_ref[...] = _bf(kv[:, HEAD_DIM:])
    o = q_w + V7X_LANES
    qi = proj[:, o:o + qi_w]
    qi_ref[...] = _rope(qi, jnp.tile(cos2, (1, qi_w // V7X_LANES)), jnp.tile(sin2, (1, qi_w // V7X_LANES)))
    o = o + qi_w
    kiw = proj[:, o:o + V7X_LANES]
    in_ki = lax.broadcasted_iota(jnp.int32, kiw.shape, 1) < IDX_DIM
    mu = jnp.sum(kiw, axis=-1, keepdims=True) * (1.0 / IDX_DIM)
    d = jnp.where(in_ki, kiw - mu, 0.0)
    var = jnp.sum(d * d, axis=-1, keepdims=True) * (1.0 / IDX_DIM)
    ki = d * lax.rsqrt(var + LN_EPS) * gb_ref[0:1, :] + gb_ref[1:2, :]
    ki_ref[...] = _rope(ki, cos2, sin2)[:, :IDX_DIM]
    o = o + V7X_LANES
    wi_ref[...] = proj[:, o:o + IDX_HEADS] * IDX_W_SCALE


def _dsa_proj(x2d, seq_len, w_in, idx_g, idx_b, cos2, sin2):
    n_tok, c = x2d.shape
    q_w = N_HEADS * HEAD_DIM
    qi_w = IDX_HEADS * IDX_DIM
    o_ki = q_w + 2 * HEAD_DIM + qi_w
    w_pad = jnp.concatenate([w_in[:, :o_ki + IDX_DIM], jnp.zeros((c, V7X_LANES - IDX_DIM), F32),
                             w_in[:, o_ki + IDX_DIM:], jnp.zeros((c, V7X_LANES - IDX_HEADS), F32)], axis=1)
    tm = min(ROW_TILE, seq_len)
    tiles = seq_len // tm
    gb = jnp.zeros((V7X_SUBLANES, V7X_LANES), F32).at[0, :IDX_DIM].set(idx_g).at[1, :IDX_DIM].set(idx_b)
    row = lambda w: pl.BlockSpec((tm, w), lambda i: (i, 0))
    tab = pl.BlockSpec((tm, V7X_LANES), lambda i: (i % tiles, 0))
    shapes = [(q_w, BF16), (HEAD_DIM, BF16), (HEAD_DIM, BF16), (qi_w, F32), (IDX_DIM, F32), (IDX_HEADS, F32)]
    return pl.pallas_call(
        _dsa_proj_kernel,
        grid=(n_tok // tm,),
        in_specs=[row(c), _full(w_pad.shape), tab, tab, _full(gb.shape)],
        out_specs=[row(w) for w, _ in shapes],
        out_shape=[jax.ShapeDtypeStruct((n_tok, w), dt) for w, dt in shapes],
        compiler_params=_params("parallel"),
        name="dsa_proj",
    )(x2d, _bf(w_pad), cos2, sin2, gb)


def _dsa_attn_kernel(x_ref, q_ref, qi_ref, wi_ref, ki_ref, k_ref, v_ref, wo_ref, gb_ref, o_ref, *, q_block0, n_keys, k_sel):
    qb = q_block0 + pl.program_id(1)
    tq = q_ref.shape[0]
    nk = n_keys

    ki_hi, ki_lo = _split2(ki_ref[0:nk, :])
    ki3 = jnp.concatenate([ki_hi, ki_hi, ki_lo], axis=1)
    qi = qi_ref[...]
    wi = wi_ref[...]
    s = jnp.zeros((tq, nk), F32)
    for h in range(IDX_HEADS):
        q_hi, q_lo = _split2(qi[:, h * IDX_DIM:(h + 1) * IDX_DIM])
        sh = _dot_nt(jnp.concatenate([q_hi, q_lo, q_hi], axis=1), ki3)
        s = s + jnp.maximum(sh, 0.0) * wi[:, h:h + 1]

    key_pos = lax.broadcasted_iota(jnp.int32, (tq, nk), 1)
    q_pos = qb * tq + lax.broadcasted_iota(jnp.int32, (tq, nk), 0)
    causal = key_pos <= q_pos

    bits = pltpu.bitcast(s, jnp.int32)
    keys = jnp.where(causal, bits ^ ((bits >> 31) & jnp.int32(0x7FFFFFFF)), jnp.int32(INT_MIN))

    def bit_step(i, tau):
        cand = tau | jnp.left_shift(jnp.int32(1), 31 - i)
        cnt = jnp.sum(jnp.where(keys >= (cand ^ jnp.int32(INT_MIN)), 1.0, 0.0), axis=1, keepdims=True)
        return jnp.where(cnt >= k_sel, cand, tau)

    tau = lax.fori_loop(0, 32, bit_step, jnp.zeros((tq, 1), jnp.int32)) ^ jnp.int32(INT_MIN)
    above = keys > tau
    tied = keys == tau
    room = k_sel - jnp.sum(jnp.where(above, 1.0, 0.0), axis=1, keepdims=True)
    tied_bf = jnp.where(tied, 1.0, 0.0).astype(BF16)
    cr = lax.broadcasted_iota(jnp.int32, (V7X_LANES, V7X_LANES), 0)
    cc = lax.broadcasted_iota(jnp.int32, (V7X_LANES, V7X_LANES), 1)
    earlier_bf = jnp.where(cr < cc, 1.0, 0.0).astype(BF16)
    carry = jnp.zeros((tq, 1), F32)
    before = []
    for ch in range(nk // V7X_LANES):
        t = tied_bf[:, ch * V7X_LANES:(ch + 1) * V7X_LANES]
        before.append(_dot(t, earlier_bf) + carry)
        carry = carry + jnp.sum(t.astype(F32), axis=1, keepdims=True)
    before = jnp.concatenate(before, axis=1)
    sel = jnp.logical_and(jnp.logical_or(above, jnp.logical_and(tied, before < room)), causal)
    bias = jnp.where(sel, 0.0, -1e30)

    k = k_ref[0:nk, :]
    v = v_ref[0:nk, :]
    q = q_ref[...] * (HEAD_DIM ** -0.5)
    n_heads = q.shape[1] // HEAD_DIM
    outs = []
    for g0 in range(0, n_heads, DSA_HEAD_GROUP):
        hs = range(g0, g0 + DSA_HEAD_GROUP)
        logits = _dot_nt(jnp.concatenate([q[:, h * HEAD_DIM:(h + 1) * HEAD_DIM] for h in hs], axis=0), k)
        ps, inv = [], []
        for i in range(DSA_HEAD_GROUP):
            lg = logits[i * tq:(i + 1) * tq, :] + bias
            p = jnp.exp(lg - jnp.max(lg, axis=1, keepdims=True))
            inv.append(1.0 / jnp.sum(p, axis=1, keepdims=True))
            ps.append(_bf(p))
        o = _dot(jnp.concatenate(ps, axis=0), v)
        outs += [o[i * tq:(i + 1) * tq, :] * inv[i] for i in range(DSA_HEAD_GROUP)]
    att = jnp.concatenate(outs, axis=1)
    m_out = _dot(_bf(att), wo_ref[...])
    o_ref[...] = _layer_norm(DEEPNORM_ALPHA * x_ref[...] + m_out, gb_ref[0:1, :], gb_ref[1:2, :])


def _dsa_attn(x2d, batch, seq_len, q, k, v, qi, ki, wi, w_o, ln_g, ln_b):
    n_tok, c = x2d.shape
    tq = Q_TILE
    k_sel = min(TOPK_MAX, seq_len // 4)
    kg = min(KEY_GROUP, seq_len)
    assert seq_len % kg == 0 and kg % tq == 0
    qb_per_group = kg // tq
    qb_per_seq = seq_len // tq
    ki3 = ki.reshape(batch, seq_len, IDX_DIM)
    k3 = k.reshape(batch, seq_len, HEAD_DIM)
    v3 = v.reshape(batch, seq_len, HEAD_DIM)
    gb = jnp.zeros((V7X_SUBLANES, c), F32).at[0].set(ln_g).at[1].set(ln_b)
    wo = _bf(w_o)
    outs = []
    for grp in range(seq_len // kg):
        q0 = grp * qb_per_group
        n_keys = (grp + 1) * kg
        row = lambda w: pl.BlockSpec((tq, w), lambda bi, j: (bi * qb_per_seq + q0 + j, 0))
        per_batch = lambda w: pl.BlockSpec((None, seq_len, w), lambda bi, j: (bi, 0, 0))
        out = pl.pallas_call(
            functools.partial(_dsa_attn_kernel, q_block0=q0, n_keys=n_keys, k_sel=k_sel),
            grid=(batch, qb_per_group),
            in_specs=[row(c), row(q.shape[1]), row(qi.shape[1]), row(wi.shape[1]),
                      per_batch(IDX_DIM), per_batch(HEAD_DIM), per_batch(HEAD_DIM),
                      _full(wo.shape), _full(gb.shape)],
            out_specs=pl.BlockSpec((None, tq, c), lambda bi, j: (bi, j, 0)),
            out_shape=jax.ShapeDtypeStruct((batch, kg, c), F32),
            compiler_params=_params("parallel", "parallel"),
            name=f"dsa_attn_{n_keys}",
        )(x2d, q, qi, wi, ki3, k3, v3, wo, gb)
        outs.append(out)
    return jnp.concatenate(outs, axis=1).reshape(n_tok, c)


def _rope_tables(seq_len):
    pos = jnp.arange(seq_len, dtype=F32)
    inv = 1.0 / (ROPE_THETA ** (jnp.arange(0, HEAD_DIM, 2, dtype=F32) / HEAD_DIM))
    ang = pos[:, None] * inv[None, :]
    cos, sin = jnp.cos(ang), jnp.sin(ang)
    cos2 = jnp.tile(jnp.concatenate([cos, cos], axis=1), (1, V7X_LANES // HEAD_DIM))
    sin2 = jnp.tile(jnp.concatenate([-sin, sin], axis=1), (1, V7X_LANES // HEAD_DIM))
    return cos2, sin2


def kernel(x, rw_mu, rw_w_rkv, rw_w0, rw_w1, rw_w2, rw_a0, rw_a1, rw_a2, rw_g1, rw_g2, rw_k_k, rw_k_a, rw_r_k, rw_lnx_g, rw_lnx_b, rw_w_o, ds_w_in, ds_idx_ln_g, ds_idx_ln_b, ds_w_o, ln_mix_g, ln_mix_b, moe_w_router, moe_b_router, moe_w_gu, moe_b_gu, moe_w_down, moe_b_down, ln_ffn_g, ln_ffn_b):
    batch, seq_len, c = x.shape
    assert c == D_MODEL
    h = x.reshape(batch * seq_len, c)
    cos2, sin2 = _rope_tables(seq_len)
    head_of = jnp.arange(2 * V7X_LANES, dtype=jnp.int32) // HEAD_DIM
    bd = (head_of[:, None] == head_of[None, :]).astype(BF16)
    n_mixers = 2
    for i in range(DEPTH):
        j = i // n_mixers
        if i % n_mixers == 0:
            r, lw, k, v, na, b, g = _rwkv_pre(h, seq_len, rw_mu[j], rw_w_rkv[j], rw_w0[j], rw_w1[j], rw_w2[j],
                                              rw_a0[j], rw_a1[j], rw_a2[j], rw_g1[j], rw_g2[j], rw_k_k[j], rw_k_a[j], bd)
            z = _rwkv_scan(r, lw, k, v, na, b, g, rw_lnx_g[j], rw_lnx_b[j], rw_r_k[j], batch, seq_len)
            h = _proj_ln(h, z, rw_w_o[j], ln_mix_g[i], ln_mix_b[i])
        else:
            q, k, v, qi, ki, wi = _dsa_proj(h, seq_len, ds_w_in[j], ds_idx_ln_g[j], ds_idx_ln_b[j], cos2, sin2)
            h = _dsa_attn(h, batch, seq_len, q, k, v, qi, ki, wi, ds_w_o[j], ln_mix_g[i], ln_mix_b[i])
        h = _moe(h, moe_w_router[i], moe_b_router[i], moe_w_gu, moe_b_gu, moe_w_down, moe_b_down,
                 ln_ffn_g[i], ln_ffn_b[i], i)
    return h.reshape(batch, seq_len, c)
```

```python
import functools
import math

import jax
import jax.numpy as jnp
from jax import lax
from jax.experimental import pallas as pl
from jax.experimental.pallas import tpu as pltpu

D_MODEL = 1024
HEAD_DIM = 64
N_HEADS = D_MODEL // HEAD_DIM
RW_GN_EPS = 64e-5
IDX_HEADS = 8
IDX_DIM = 64
TOPK_MAX = 256
ROPE_THETA = 10000.0
N_EXPERTS = 32
TOP_K = 4
SWIGLU_LIMIT = 7.0
SWIGLU_ALPHA = 1.702
LN_EPS = 1e-5
DEPTH = 2
DEEPNORM_ALPHA = (2.0 * DEPTH) ** 0.25
IDX_W_SCALE = IDX_HEADS ** -0.5 * IDX_DIM ** -0.5

V7X_LANES = 128
V7X_SUBLANES = 8
V7X_VMEM_LIMIT_BYTES = 56 * 1024 * 1024

SCAN_CHUNK = 64
SCAN_SUB = 2
ROW_TILE = 256
Q_TILE = 128
KEY_GROUP = 512
MOE_TILE = 512
MOE_ROUTER_TILE = 1024
DSA_HEAD_GROUP = 4

F32 = jnp.float32
BF16 = jnp.bfloat16
INT_MIN = -2147483648


def _bf(x):
    return x.astype(BF16)


def _split2(x):
    hi = x.astype(BF16)
    lo = (x - hi.astype(F32)).astype(BF16)
    return hi, lo


def _dot(a, b):
    return jnp.dot(a, b, preferred_element_type=F32)


def _dot_nt(a, b):
    return lax.dot_general(a, b, (((1,), (1,)), ((), ())), preferred_element_type=F32)


def _dot_tn(a, b):
    return lax.dot_general(a, b, (((0,), (0,)), ((), ())), preferred_element_type=F32)


def _layer_norm(h, g, b):
    mu = jnp.mean(h, axis=-1, keepdims=True)
    d = h - mu
    var = jnp.mean(d * d, axis=-1, keepdims=True)
    return d * lax.rsqrt(var + LN_EPS) * g + b


def _sigmoid(z):
    return 1.0 / (1.0 + jnp.exp(-z))


def _params(*sem):
    return pltpu.CompilerParams(dimension_semantics=sem, vmem_limit_bytes=V7X_VMEM_LIMIT_BYTES)


def _full(shape):
    n = len(shape)
    return pl.BlockSpec(shape, lambda *_: (0,) * n)


def _rwkv_pre_kernel(x_ref, xp_ref, mu_ref, wrkv_ref, w1_ref, w2_ref, a1_ref, a2_ref, g1_ref, g2_ref,
                     vec_ref, bd_ref, r_ref, lw_ref, k_ref, v_ref, na_ref, b_ref, g_ref, *, tiles_per_seq):
    i = pl.program_id(0)
    x = x_ref[...]
    prev = xp_ref[V7X_SUBLANES - 1:V7X_SUBLANES, :]
    prev = jnp.where(i % tiles_per_seq == 0, 0.0, prev)
    rows = lax.broadcasted_iota(jnp.int32, x.shape, 0)
    xprev = jnp.where(rows == 0, prev, pltpu.roll(x, 1, 0))
    xx = xprev - x

    def mix(p):
        return x + xx * mu_ref[p:p + 1, :]

    w0 = vec_ref[0:1, :]
    a0 = vec_ref[1:2, :]
    k_k = vec_ref[2:3, :]
    k_a = vec_ref[3:4, :]

    r = _dot(_bf(mix(0)), wrkv_ref[0])
    k = _dot(_bf(mix(1)), wrkv_ref[1])
    v = _dot(_bf(mix(2)), wrkv_ref[2])

    zw = w0 + _dot(_bf(jnp.tanh(_dot(_bf(mix(3)), w1_ref[...]))), w2_ref[...])
    nz = -zw
    softplus = jnp.maximum(nz, 0.0) + jnp.log1p(jnp.exp(-jnp.abs(nz)))
    w_log = -softplus - 0.5
    lw = -jnp.exp(w_log)

    a = _sigmoid(a0 + _dot(_bf(_dot(_bf(mix(4)), a1_ref[...])), a2_ref[...]))
    g = _dot(_bf(_sigmoid(_dot(_bf(mix(5)), g1_ref[...]))), g2_ref[...])

    kk = k * k_k
    sq_hi, sq_lo = _split2(kk * kk)
    gw = bd_ref.shape[0]
    ss = jnp.concatenate([_dot(sq_hi[:, c0:c0 + gw], bd_ref[...]) + _dot(sq_lo[:, c0:c0 + gw], bd_ref[...])
                          for c0 in range(0, kk.shape[1], gw)], axis=1)
    kk = kk / jnp.maximum(jnp.sqrt(ss), 1e-12)
    k = k * (1.0 + (a - 1.0) * k_a)

    r_ref[...] = r
    lw_ref[...] = lw
    k_ref[...] = k
    v_ref[...] = v
    na_ref[...] = -kk
    b_ref[...] = kk * a
    g_ref[...] = g


def _rwkv_pre(x2d, seq_len, mu, w_rkv, w0, w1, w2, a0, a1, a2, g1, g2, k_k, k_a, bd):
    n_tok, c = x2d.shape
    tm = min(ROW_TILE, seq_len)
    assert seq_len % tm == 0 and tm % V7X_SUBLANES == 0
    vecs = jnp.zeros((V7X_SUBLANES, c), F32).at[0].set(w0).at[1].set(a0).at[2].set(k_k).at[3].set(k_a)
    mu8 = jnp.zeros((V7X_SUBLANES, c), F32).at[:6].set(mu)
    row = pl.BlockSpec((tm, c), lambda i: (i, 0))
    sub_per_tile = tm // V7X_SUBLANES
    prev = pl.BlockSpec((V7X_SUBLANES, c), lambda i: (jnp.maximum(i * sub_per_tile - 1, 0), 0))
    ws = [_bf(w_rkv), _bf(w1), _bf(w2), _bf(a1), _bf(a2), _bf(g1), _bf(g2)]
    out = jax.ShapeDtypeStruct((n_tok, c), F32)
    return pl.pallas_call(
        functools.partial(_rwkv_pre_kernel, tiles_per_seq=seq_len // tm),
        grid=(n_tok // tm,),
        in_specs=[row, prev, _full(mu8.shape)] + [_full(w.shape) for w in ws] + [_full(vecs.shape), _full(bd.shape)],
        out_specs=[row] * 7,
        out_shape=[out] * 7,
        compiler_params=_params("parallel"),
        name="rwkv_pre",
    )(x2d, x2d, mu8, *ws, vecs, bd)


def _rwkv_scan_kernel(r_ref, lw_ref, k_ref, v_ref, na_ref, b_ref, g_ref, vec_ref, o_ref, s_ref):
    R, C = r_ref.shape
    L = SCAN_CHUNK
    N = HEAD_DIM
    PW = 2 * N
    assert PW == V7X_LANES and 2 * L == PW and R % L == 0
    subs = range(R // L)
    rs = [slice(c * L, (c + 1) * L) for c in subs]

    @pl.when(pl.program_id(1) == 0)
    def _():
        s_ref[...] = jnp.zeros_like(s_ref)

    def head_a(width, n_rows=L):
        return (lax.broadcasted_iota(jnp.int32, (n_rows, width), 1) & N) == 0

    def split_ab(z):
        m = head_a(z.shape[1])
        return jnp.concatenate([jnp.where(m, z, 0.0), jnp.where(m, 0.0, z)], axis=0)

    def split_ba(z):
        m = head_a(z.shape[1])
        return jnp.concatenate([jnp.where(m, 0.0, z), jnp.where(m, z, 0.0)], axis=0)

    rows = lax.broadcasted_iota(jnp.int32, (R, R), 0)
    cols = lax.broadcasted_iota(jnp.int32, (R, R), 1)
    lw = lw_ref[...]
    tri = jnp.where(jnp.logical_and(rows >= cols, (rows ^ cols) < L), 1.0, 0.0).astype(BF16)
    h1 = lw.astype(BF16)
    r1 = lw - h1.astype(F32)
    h2 = r1.astype(BF16)
    h3 = (r1 - h2.astype(F32)).astype(BF16)
    cs = _dot(tri, h1) + _dot(tri, h2) + _dot(tri, h3)
    cs_end = [cs[(c + 1) * L - 1:(c + 1) * L, :] for c in subs]
    g_incl = jnp.exp(cs)
    g_excl = jnp.exp(cs - lw)
    g_inv = jnp.exp(-cs)
    g_tail = jnp.exp(jnp.concatenate([jnp.broadcast_to(e, (L, C)) for e in cs_end], axis=0) - cs)
    g_last = [jnp.exp(e) for e in cs_end]

    r = r_ref[...]
    k = k_ref[...]
    v = v_ref[...]
    b = b_ref[...]
    at = na_ref[...] * g_excl
    rt = r * g_incl
    bt = b * g_inv
    kt = k * g_inv
    bh = b * g_tail
    kh = k * g_tail

    n_pairs = C // PW
    units = range(len(subs) * n_pairs)
    sls = [(rs[i // n_pairs], slice((i % n_pairs) * PW, (i % n_pairs + 1) * PW)) for i in units]
    pairs = units
    t_idx = lax.broadcasted_iota(jnp.int32, (L, PW), 0)
    s_idx = lax.broadcasted_iota(jnp.int32, (L, PW), 1) & (N - 1)
    strict2 = t_idx > s_idx
    incl2 = t_idx >= s_idx
    is_a = head_a(PW)
    is_a2 = head_a(PW, 2 * L)

    lhs = [jnp.concatenate([at[sl], rt[sl]], axis=0) for sl in sls]
    g_a = [_dot_nt(_bf(jnp.where(is_a2, lhs[j], 0.0)), _bf(jnp.concatenate([bt[sls[j]], kt[sls[j]]], axis=0)))
           for j in pairs]
    g_b = [_dot_nt(_bf(jnp.where(is_a2, 0.0, lhs[j])), _bf(jnp.concatenate([kt[sls[j]], bt[sls[j]]], axis=0)))
           for j in pairs]
    pp = [jnp.where(strict2, jnp.where(is_a, g_a[j][:L], g_b[j][:L]), 0.0) for j in pairs]
    m_ak = [jnp.where(strict2, jnp.where(is_a, g_b[j][:L], g_a[j][:L]), 0.0) for j in pairs]
    w_rb = [jnp.where(incl2, jnp.where(is_a, g_a[j][L:], g_b[j][L:]), 0.0) for j in pairs]
    w_rk = [jnp.where(incl2, jnp.where(is_a, g_b[j][L:], g_a[j][L:]), 0.0) for j in pairs]
    mv = [_dot(_bf(m_ak[j]), _bf(split_ba(v[sls[j]]))) for j in pairs]
    xs = [jnp.concatenate([at[sls[j]], mv[j]], axis=1) for j in pairs]
    levels = int(math.log2(L))
    for lvl in range(levels):
        if lvl < levels - 1:
            z = [_dot(_bf(pp[j]), _bf(split_ab(jnp.concatenate([xs[j], pp[j]], axis=1)))) for j in pairs]
            xs = [xs[j] + z[j][:, :2 * PW] for j in pairs]
            pp = [z[j][:, 2 * PW:] for j in pairs]
        else:
            xs = [xs[j] + _dot(_bf(pp[j]), _bf(split_ab(xs[j]))) for j in pairs]
    zy = [_dot(_bf(jnp.concatenate([w_rb[j], w_rk[j]], axis=1)),
               _bf(jnp.concatenate([split_ab(xs[j]),
                                    split_ba(jnp.concatenate([jnp.zeros((L, PW), F32), v[sls[j]]], axis=1))], axis=0)))
          for j in pairs]
    vr = lax.broadcasted_iota(jnp.int32, (PW, PW), 0)
    kc = lax.broadcasted_iota(jnp.int32, (PW, PW), 1)
    same_head = (vr & N) == (kc & N)
    state = [s_ref[p] for p in range(n_pairs)]
    y_rows = []
    for c in subs:
        ids = [c * n_pairs + p for p in range(n_pairs)]
        uy = [_dot_nt(_bf(jnp.concatenate([xs[j][:, :PW], rt[sls[j]] + zy[j][:, :PW]], axis=0)), _bf(state[j % n_pairs]))
              for j in ids]
        u = [uy[p][:L] + xs[j][:, PW:] for p, j in enumerate(ids)]
        y_rows.append(jnp.concatenate([uy[p][L:] + zy[j][:, PW:] for p, j in enumerate(ids)], axis=1))
        upd = [_dot_tn(_bf(jnp.concatenate([u[p], v[sls[j]]], axis=0)),
                       _bf(jnp.concatenate([bh[sls[j]], kh[sls[j]]], axis=0))) for p, j in enumerate(ids)]
        state = [state[p] * g_last[c][:, sls[j][1]] + jnp.where(same_head, upd[p], 0.0) for p, j in enumerate(ids)]
    for p in range(n_pairs):
        s_ref[p] = state[p]

    yy = jnp.concatenate(y_rows, axis=0)
    gw = 2 * PW
    jr = lax.broadcasted_iota(jnp.int32, (gw, gw), 0)
    jc = lax.broadcasted_iota(jnp.int32, (gw, gw), 1)
    ones_bd = jnp.where((jr ^ jc) < N, 1.0, 0.0).astype(BF16)

    def head_sum(z):
        hi, lo = _split2(z)
        return jnp.concatenate([_dot(hi[:, c0:c0 + gw], ones_bd) + _dot(lo[:, c0:c0 + gw], ones_bd)
                                for c0 in range(0, C, gw)], axis=1)

    yd = yy - head_sum(yy) * (1.0 / N)
    yv = head_sum(yd * yd) * (1.0 / N)
    yn = yd * lax.rsqrt(yv + RW_GN_EPS) * vec_ref[0:1, :] + vec_ref[1:2, :]
    bonus = head_sum(r * k * vec_ref[2:3, :]) * v
    o_ref[...] = (yn + bonus) * g_ref[...]


def _rwkv_scan(r, lw, k, v, na, b, g, lnx_g, lnx_b, r_k, batch, seq_len):
    n_tok, c = r.shape
    rows = SCAN_CHUNK * SCAN_SUB
    assert seq_len % rows == 0
    vecs = jnp.zeros((V7X_SUBLANES, c), F32).at[0].set(lnx_g).at[1].set(lnx_b).at[2].set(r_k.reshape(c))
    chunks = seq_len // rows
    blk = pl.BlockSpec((rows, c), lambda bi, ci: (bi * chunks + ci, 0))
    return pl.pallas_call(
        _rwkv_scan_kernel,
        grid=(batch, chunks),
        in_specs=[blk] * 7 + [_full(vecs.shape)],
        out_specs=blk,
        out_shape=jax.ShapeDtypeStruct((n_tok, c), F32),
        scratch_shapes=[pltpu.VMEM((c // V7X_LANES, V7X_LANES, V7X_LANES), F32)],
        compiler_params=_params("parallel", "arbitrary"),
        name="rwkv_scan",
    )(r, lw, k, v, na, b, g, vecs)


def _proj_ln_kernel(x_ref, z_ref, w_ref, gb_ref, o_ref):
    m = _dot(_bf(z_ref[...]), w_ref[...])
    o_ref[...] = _layer_norm(DEEPNORM_ALPHA * x_ref[...] + m, gb_ref[0:1, :], gb_ref[1:2, :])


def _proj_ln(x2d, z2d, w, ln_g, ln_b):
    n_tok, c = x2d.shape
    tm = min(2 * ROW_TILE, n_tok)
    gb = jnp.zeros((V7X_SUBLANES, c), F32).at[0].set(ln_g).at[1].set(ln_b)
    row = pl.BlockSpec((tm, c), lambda i: (i, 0))
    return pl.pallas_call(
        _proj_ln_kernel,
        grid=(n_tok // tm,),
        in_specs=[row, row, _full(w.shape), _full(gb.shape)],
        out_specs=row,
        out_shape=jax.ShapeDtypeStruct((n_tok, c), F32),
        compiler_params=_params("parallel"),
        name="proj_ln",
    )(x2d, z2d, _bf(w), gb)


def _router_kernel(x_ref, wt_ref, b_ref, earlier_ref, e_ref, gate_ref, rank_ref, cnt_ref):
    x_hi, x_lo = _split2(x_ref[...])
    w_hi, w_lo = _split2(wt_ref[...])
    logits = _dot_nt(w_hi, x_hi) + _dot_nt(w_hi, x_lo) + _dot_nt(w_lo, x_hi) + b_ref[...]
    n_e = logits.shape[0]
    eid = lax.broadcasted_iota(jnp.int32, logits.shape, 0)
    tops, idxs, hots = [], [], []
    for _ in range(TOP_K):
        m = jnp.max(logits, axis=0, keepdims=True)
        idx = jnp.min(jnp.where(logits == m, eid, n_e), axis=0, keepdims=True)
        tops.append(m)
        idxs.append(idx)
        hots.append(eid == idx)
        logits = jnp.where(eid == idx, -jnp.inf, logits)
    ex = [jnp.exp(t - tops[0]) for t in tops]
    denom = ex[0] + ex[1] + ex[2] + ex[3]
    e_ref[...] = jnp.concatenate(idxs, axis=0)
    gate_ref[...] = jnp.concatenate([e / denom for e in ex], axis=0)
    chosen = sum(jnp.where(hot, 1.0, 0.0) for hot in hots)
    before = _dot(_bf(chosen), earlier_ref[...])
    rank_ref[...] = jnp.concatenate([jnp.sum(jnp.where(hot, before, 0.0), axis=0, keepdims=True) for hot in hots],
                                    axis=0).astype(jnp.int32)
    cnt_ref[...] = jnp.broadcast_to(jnp.sum(chosen, axis=1, keepdims=True), cnt_ref.shape).astype(jnp.int32)


def _router(x2d, w_router, b_router):
    n_tok, c = x2d.shape
    tm = min(MOE_ROUTER_TILE, n_tok)
    wt = w_router.T
    bcol = b_router.reshape(N_EXPERTS, 1)
    tok = jnp.arange(tm, dtype=jnp.int32)
    earlier = (tok[:, None] < tok[None, :]).astype(BF16)
    tile_of = lambda i: (0, i)
    return pl.pallas_call(
        _router_kernel,
        grid=(n_tok // tm,),
        in_specs=[pl.BlockSpec((tm, c), lambda i: (i, 0)), _full(wt.shape), _full(bcol.shape), _full(earlier.shape)],
        out_specs=[pl.BlockSpec((TOP_K, tm), tile_of)] * 3 + [pl.BlockSpec((None, N_EXPERTS, V7X_LANES), lambda i: (i, 0, 0))],
        out_shape=[jax.ShapeDtypeStruct((TOP_K, n_tok), jnp.int32), jax.ShapeDtypeStruct((TOP_K, n_tok), F32),
                   jax.ShapeDtypeStruct((TOP_K, n_tok), jnp.int32),
                   jax.ShapeDtypeStruct((n_tok // tm, N_EXPERTS, V7X_LANES), jnp.int32)],
        compiler_params=_params("parallel"),
        name="moe_router",
    )(x2d, wt, bcol, earlier)


def _expert_kernel(blk_e_ref, n_used_ref, x_ref, wgu_ref, bgu_ref, wd_ref, bd_ref, y_ref, wgu_bf, wd_bf):
    i = pl.program_id(0)
    e = blk_e_ref[i]
    changed = jnp.logical_or(i == 0, e != blk_e_ref[jnp.maximum(i - 1, 0)])

    @pl.when(changed)
    def _():
        wgu_bf[...] = _bf(wgu_ref[...])
        wd_bf[...] = _bf(wd_ref[...])

    @pl.when(i < n_used_ref[0])
    def _():
        f = wd_ref.shape[0]
        gu = _dot(_bf(x_ref[...]), wgu_bf[...]) + bgu_ref[...]
        glu = jnp.minimum(gu[:, :f], SWIGLU_LIMIT)
        lin = jnp.clip(gu[:, f:], -SWIGLU_LIMIT, SWIGLU_LIMIT)
        act = glu * _sigmoid(SWIGLU_ALPHA * glu) * (lin + 1.0)
        y_ref[...] = _dot(_bf(act), wd_bf[...]) + bd_ref[...]

    @pl.when(i >= n_used_ref[0])
    def _():
        y_ref[...] = jnp.zeros_like(y_ref)


def _experts(x_rows, blk_e, n_used, w_gu, b_gu, w_down, b_down, layer):
    n_rows, c = x_rows.shape
    n_l, n_e, _, f2 = w_gu.shape
    f = f2 // 2
    tb = MOE_TILE
    grid_spec = pltpu.PrefetchScalarGridSpec(
        num_scalar_prefetch=2,
        grid=(n_rows // tb,),
        in_specs=[
            pl.BlockSpec((tb, c), lambda i, be, nu: (jnp.minimum(i, nu[0] - 1), 0)),
            pl.BlockSpec((None, None, c, f2), lambda i, be, nu: (layer, be[i], 0, 0)),
            pl.BlockSpec((None, None, 1, f2), lambda i, be, nu: (layer, be[i], 0, 0)),
            pl.BlockSpec((None, None, f, c), lambda i, be, nu: (layer, be[i], 0, 0)),
            pl.BlockSpec((None, None, 1, c), lambda i, be, nu: (layer, be[i], 0, 0)),
        ],
        out_specs=pl.BlockSpec((tb, c), lambda i, be, nu: (i, 0)),
        scratch_shapes=[pltpu.VMEM((c, f2), BF16), pltpu.VMEM((f, c), BF16)],
    )
    return pl.pallas_call(
        _expert_kernel,
        grid_spec=grid_spec,
        out_shape=jax.ShapeDtypeStruct((n_rows, c), F32),
        compiler_params=_params("arbitrary"),
        name="moe_experts",
    )(blk_e, n_used, x_rows, w_gu, b_gu.reshape(n_l, n_e, 1, f2), w_down, b_down.reshape(n_l, n_e, 1, c))


def _combine_ln_kernel(x_ref, y_ref, gate_ref, gb_ref, o_ref):
    acc = DEEPNORM_ALPHA * x_ref[...]
    for j in range(TOP_K):
        acc = acc + gate_ref[:, j:j + 1] * y_ref[j]
    o_ref[...] = _layer_norm(acc, gb_ref[0:1, :], gb_ref[1:2, :])


def _combine_ln(x2d, y4, gates, ln_g, ln_b):
    n_tok, c = x2d.shape
    tm = min(ROW_TILE, n_tok)
    gb = jnp.zeros((V7X_SUBLANES, c), F32).at[0].set(ln_g).at[1].set(ln_b)
    return pl.pallas_call(
        _combine_ln_kernel,
        grid=(n_tok // tm,),
        in_specs=[pl.BlockSpec((tm, c), lambda i: (i, 0)), pl.BlockSpec((TOP_K, tm, c), lambda i: (0, i, 0)),
                  pl.BlockSpec((tm, TOP_K), lambda i: (i, 0)), _full(gb.shape)],
        out_specs=pl.BlockSpec((tm, c), lambda i: (i, 0)),
        out_shape=jax.ShapeDtypeStruct((n_tok, c), F32),
        compiler_params=_params("parallel"),
        name="moe_combine_ln",
    )(x2d, y4, gates, gb)


def _moe(x2d, w_router, b_router, w_gu, b_gu, w_down, b_down, ln_g, ln_b, layer):
    n_tok, c = x2d.shape
    top_e, gates, rank, tile_cnt = _router(x2d, w_router, b_router)
    tb = MOE_TILE
    tile_cnt = tile_cnt[:, :, 0]
    n_tiles = tile_cnt.shape[0]
    counts = jnp.sum(tile_cnt, axis=0)
    padded = (counts + tb - 1) // tb * tb
    pad_end = jnp.cumsum(padded)
    pad_start = pad_end - padded
    base = pad_start[None, :] + jnp.cumsum(tile_cnt, axis=0) - tile_cnt
    base_tok = jnp.broadcast_to(base[:, None, :], (n_tiles, n_tok // n_tiles, N_EXPERTS)).reshape(n_tok, N_EXPERTS)
    hot = top_e[:, :, None] == jnp.arange(N_EXPERTS, dtype=jnp.int32)
    dest = jnp.sum(jnp.where(hot, base_tok[None], 0), axis=-1) + rank
    n_rows = -(-(n_tok * TOP_K + N_EXPERTS * (tb - 1)) // tb) * tb
    blk_start = jnp.arange(n_rows // tb, dtype=jnp.int32) * tb
    blk_e = jnp.minimum(jnp.sum((pad_end[None, :] <= blk_start[:, None]).astype(jnp.int32), axis=1), N_EXPERTS - 1)
    n_used = (pad_end[-1] // tb).astype(jnp.int32).reshape(1)
    tok = jnp.broadcast_to(jnp.arange(n_tok, dtype=jnp.int32)[None, :], (TOP_K, n_tok))
    rows_tok = jnp.zeros((n_rows,), jnp.int32).at[dest.reshape(-1)].set(tok.reshape(-1))
    x_rows = x2d[rows_tok]
    y_rows = _experts(x_rows, blk_e, n_used, w_gu, b_gu, w_down, b_down, layer)
    y4 = y_rows[dest.reshape(-1)].reshape(TOP_K, n_tok, c)
    return _combine_ln(x2d, y4, gates.T, ln_g, ln_b)


def _rope(z, cos, sin_signed):
    half = HEAD_DIM // 2
    width = z.shape[-1]
    lane = lax.broadcasted_iota(jnp.int32, z.shape, z.ndim - 1)
    swapped = jnp.where(lane % HEAD_DIM < half, pltpu.roll(z, width - half, z.ndim - 1), pltpu.roll(z, half, z.ndim - 1))
    return z * cos + swapped * sin_signed


def _dsa_proj_kernel(x_ref, w_ref, cos_ref, sin_ref, gb_ref, q_ref, k_ref, v_ref, qi_ref, ki_ref, wi_ref):
    c = x_ref.shape[1]
    proj = _dot(_bf(x_ref[...]), w_ref[...])
    cos2 = cos_ref[...]
    sin2 = sin_ref[...]
    q_w = q_ref.shape[1]
    qi_w = qi_ref.shape[1]
    q = proj[:, :q_w]
    q_ref[...] = _bf(_rope(q, jnp.tile(cos2, (1, q_w // V7X_LANES)), jnp.tile(sin2, (1, q_w // V7X_LANES))))
    kv = proj[:, q_w:q_w + V7X_LANES]
    k_ref[...] = _bf(_rope(kv, cos2, sin2)[:, :HEAD_DIM])
    v_ref[...] = _bf(kv[:, HEAD_DIM:])
    o = q_w + V7X_LANES
    qi = proj[:, o:o + qi_w]
    qi_ref[...] = _rope(qi, jnp.tile(cos2, (1, qi_w // V7X_LANES)), jnp.tile(sin2, (1, qi_w // V7X_LANES)))
    o = o + qi_w
    kiw = proj[:, o:o + V7X_LANES]
    in_ki = lax.broadcasted_iota(jnp.int32, kiw.shape, 1) < IDX_DIM
    mu = jnp.sum(kiw, axis=-1, keepdims=True) * (1.0 / IDX_DIM)
    d = jnp.where(in_ki, kiw - mu, 0.0)
    var = jnp.sum(d * d, axis=-1, keepdims=True) * (1.0 / IDX_DIM)
    ki = d * lax.rsqrt(var + LN_EPS) * gb_ref[0:1, :] + gb_ref[1:2, :]
    ki_ref[...] = _rope(ki, cos2, sin2)[:, :IDX_DIM]
    o = o + V7X_LANES
    wi_ref[...] = proj[:, o:o + IDX_HEADS] * IDX_W_SCALE


def _dsa_proj(x2d, seq_len, w_in, idx_g, idx_b, cos2, sin2):
    n_tok, c = x2d.shape
    q_w = N_HEADS * HEAD_DIM
    qi_w = IDX_HEADS * IDX_DIM
    o_ki = q_w + 2 * HEAD_DIM + qi_w
    w_pad = jnp.concatenate([w_in[:, :o_ki + IDX_DIM], jnp.zeros((c, V7X_LANES - IDX_DIM), F32),
                             w_in[:, o_ki + IDX_DIM:], jnp.zeros((c, V7X_LANES - IDX_HEADS), F32)], axis=1)
    tm = min(ROW_TILE, seq_len)
    tiles = seq_len // tm
    gb = jnp.zeros((V7X_SUBLANES, V7X_LANES), F32).at[0, :IDX_DIM].set(idx_g).at[1, :IDX_DIM].set(idx_b)
    row = lambda w: pl.BlockSpec((tm, w), lambda i: (i, 0))
    tab = pl.BlockSpec((tm, V7X_LANES), lambda i: (i % tiles, 0))
    shapes = [(q_w, BF16), (HEAD_DIM, BF16), (HEAD_DIM, BF16), (qi_w, F32), (IDX_DIM, F32), (IDX_HEADS, F32)]
    return pl.pallas_call(
        _dsa_proj_kernel,
        grid=(n_tok // tm,),
        in_specs=[row(c), _full(w_pad.shape), tab, tab, _full(gb.shape)],
        out_specs=[row(w) for w, _ in shapes],
        out_shape=[jax.ShapeDtypeStruct((n_tok, w), dt) for w, dt in shapes],
        compiler_params=_params("parallel"),
        name="dsa_proj",
    )(x2d, _bf(w_pad), cos2, sin2, gb)


def _dsa_attn_kernel(x_ref, q_ref, qi_ref, wi_ref, ki_ref, k_ref, v_ref, wo_ref, gb_ref, o_ref, *, q_block0, n_keys, k_sel):
    qb = q_block0 + pl.program_id(1)
    tq = q_ref.shape[0]
    nk = n_keys

    ki_hi, ki_lo = _split2(ki_ref[0:nk, :])
    ki3 = jnp.concatenate([ki_hi, ki_hi, ki_lo], axis=1)
    qi = qi_ref[...]
    wi = wi_ref[...]
    s = jnp.zeros((tq, nk), F32)
    for h in range(IDX_HEADS):
        q_hi, q_lo = _split2(qi[:, h * IDX_DIM:(h + 1) * IDX_DIM])
        sh = _dot_nt(jnp.concatenate([q_hi, q_lo, q_hi], axis=1), ki3)
        s = s + jnp.maximum(sh, 0.0) * wi[:, h:h + 1]

    key_pos = lax.broadcasted_iota(jnp.int32, (tq, nk), 1)
    q_pos = qb * tq + lax.broadcasted_iota(jnp.int32, (tq, nk), 0)
    causal = key_pos <= q_pos

    bits = pltpu.bitcast(s, jnp.int32)
    keys = jnp.where(causal, bits ^ ((bits >> 31) & jnp.int32(0x7FFFFFFF)), jnp.int32(INT_MIN))

    def bit_step(i, tau):
        cand = tau | jnp.left_shift(jnp.int32(1), 31 - i)
        cnt = jnp.sum(jnp.where(keys >= (cand ^ jnp.int32(INT_MIN)), 1.0, 0.0), axis=1, keepdims=True)
        return jnp.where(cnt >= k_sel, cand, tau)

    tau = lax.fori_loop(0, 32, bit_step, jnp.zeros((tq, 1), jnp.int32)) ^ jnp.int32(INT_MIN)
    above = keys > tau
    tied = keys == tau
    room = k_sel - jnp.sum(jnp.where(above, 1.0, 0.0), axis=1, keepdims=True)
    tied_bf = jnp.where(tied, 1.0, 0.0).astype(BF16)
    cr = lax.broadcasted_iota(jnp.int32, (V7X_LANES, V7X_LANES), 0)
    cc = lax.broadcasted_iota(jnp.int32, (V7X_LANES, V7X_LANES), 1)
    earlier_bf = jnp.where(cr < cc, 1.0, 0.0).astype(BF16)
    carry = jnp.zeros((tq, 1), F32)
    before = []
    for ch in range(nk // V7X_LANES):
        t = tied_bf[:, ch * V7X_LANES:(ch + 1) * V7X_LANES]
        before.append(_dot(t, earlier_bf) + carry)
        carry = carry + jnp.sum(t.astype(F32), axis=1, keepdims=True)
    before = jnp.concatenate(before, axis=1)
    sel = jnp.logical_and(jnp.logical_or(above, jnp.logical_and(tied, before < room)), causal)
    bias = jnp.where(sel, 0.0, -1e30)

    k = k_ref[0:nk, :]
    v = v_ref[0:nk, :]
    q = q_ref[...] * (HEAD_DIM ** -0.5)
    n_heads = q.shape[1] // HEAD_DIM
    outs = []
    for g0 in range(0, n_heads, DSA_HEAD_GROUP):
        hs = range(g0, g0 + DSA_HEAD_GROUP)
        logits = _dot_nt(jnp.concatenate([q[:, h * HEAD_DIM:(h + 1) * HEAD_DIM] for h in hs], axis=0), k)
        ps, inv = [], []
        for i in range(DSA_HEAD_GROUP):
            lg = logits[i * tq:(i + 1) * tq, :] + bias
            p = jnp.exp(lg - jnp.max(lg, axis=1, keepdims=True))
            inv.append(1.0 / jnp.sum(p, axis=1, keepdims=True))
            ps.append(_bf(p))
        o = _dot(jnp.concatenate(ps, axis=0), v)
        outs += [o[i * tq:(i + 1) * tq, :] * inv[i] for i in range(DSA_HEAD_GROUP)]
    att = jnp.concatenate(outs, axis=1)
    m_out = _dot(_bf(att), wo_ref[...])
    o_ref[...] = _layer_norm(DEEPNORM_ALPHA * x_ref[...] + m_out, gb_ref[0:1, :], gb_ref[1:2, :])


def _dsa_attn(x2d, batch, seq_len, q, k, v, qi, ki, wi, w_o, ln_g, ln_b):
    n_tok, c = x2d.shape
    tq = Q_TILE
    k_sel = min(TOPK_MAX, seq_len // 4)
    kg = min(KEY_GROUP, seq_len)
    assert seq_len % kg == 0 and kg % tq == 0
    qb_per_group = kg // tq
    qb_per_seq = seq_len // tq
    ki3 = ki.reshape(batch, seq_len, IDX_DIM)
    k3 = k.reshape(batch, seq_len, HEAD_DIM)
    v3 = v.reshape(batch, seq_len, HEAD_DIM)
    gb = jnp.zeros((V7X_SUBLANES, c), F32).at[0].set(ln_g).at[1].set(ln_b)
    wo = _bf(w_o)
    outs = []
    for grp in range(seq_len // kg):
        q0 = grp * qb_per_group
        n_keys = (grp + 1) * kg
        row = lambda w: pl.BlockSpec((tq, w), lambda bi, j: (bi * qb_per_seq + q0 + j, 0))
        per_batch = lambda w: pl.BlockSpec((None, seq_len, w), lambda bi, j: (bi, 0, 0))
        out = pl.pallas_call(
            functools.partial(_dsa_attn_kernel, q_block0=q0, n_keys=n_keys, k_sel=k_sel),
            grid=(batch, qb_per_group),
            in_specs=[row(c), row(q.shape[1]), row(qi.shape[1]), row(wi.shape[1]),
                      per_batch(IDX_DIM), per_batch(HEAD_DIM), per_batch(HEAD_DIM),
                      _full(wo.shape), _full(gb.shape)],
            out_specs=pl.BlockSpec((None, tq, c), lambda bi, j: (bi, j, 0)),
            out_shape=jax.ShapeDtypeStruct((batch, kg, c), F32),
            compiler_params=_params("parallel", "parallel"),
            name=f"dsa_attn_{n_keys}",
        )(x2d, q, qi, wi, ki3, k3, v3, wo, gb)
        outs.append(out)
    return jnp.concatenate(outs, axis=1).reshape(n_tok, c)


def _rope_tables(seq_len):
    pos = jnp.arange(seq_len, dtype=F32)
    inv = 1.0 / (ROPE_THETA ** (jnp.arange(0, HEAD_DIM, 2, dtype=F32) / HEAD_DIM))
    ang = pos[:, None] * inv[None, :]
    cos, sin = jnp.cos(ang), jnp.sin(ang)
    cos2 = jnp.tile(jnp.concatenate([cos, cos], axis=1), (1, V7X_LANES // HEAD_DIM))
    sin2 = jnp.tile(jnp.concatenate([-sin, sin], axis=1), (1, V7X_LANES // HEAD_DIM))
    return cos2, sin2


def kernel(x, rw_mu, rw_w_rkv, rw_w0, rw_w1, rw_w2, rw_a0, rw_a1, rw_a2, rw_g1, rw_g2, rw_k_k, rw_k_a, rw_r_k, rw_lnx_g, rw_lnx_b, rw_w_o, ds_w_in, ds_idx_ln_g, ds_idx_ln_b, ds_w_o, ln_mix_g, ln_mix_b, moe_w_router, moe_b_router, moe_w_gu, moe_b_gu, moe_w_down, moe_b_down, ln_ffn_g, ln_ffn_b):
    batch, seq_len, c = x.shape
    assert c == D_MODEL
    h = x.reshape(batch * seq_len, c)
    cos2, sin2 = _rope_tables(seq_len)
    head_of = jnp.arange(2 * V7X_LANES, dtype=jnp.int32) // HEAD_DIM
    bd = (head_of[:, None] == head_of[None, :]).astype(BF16)
    n_mixers = 2
    for i in range(DEPTH):
        j = i // n_mixers
        if i % n_mixers == 0:
            r, lw, k, v, na, b, g = _rwkv_pre(h, seq_len, rw_mu[j], rw_w_rkv[j], rw_w0[j], rw_w1[j], rw_w2[j],
                                              rw_a0[j], rw_a1[j], rw_a2[j], rw_g1[j], rw_g2[j], rw_k_k[j], rw_k_a[j], bd)
            z = _rwkv_scan(r, lw, k, v, na, b, g, rw_lnx_g[j], rw_lnx_b[j], rw_r_k[j], batch, seq_len)
            h = _proj_ln(h, z, rw_w_o[j], ln_mix_g[i], ln_mix_b[i])
        else:
            q, k, v, qi, ki, wi = _dsa_proj(h, seq_len, ds_w_in[j], ds_idx_ln_g[j], ds_idx_ln_b[j], cos2, sin2)
            h = _dsa_attn(h, batch, seq_len, q, k, v, qi, ki, wi, ds_w_o[j], ln_mix_g[i], ln_mix_b[i])
        h = _moe(h, moe_w_router[i], moe_b_router[i], moe_w_gu, moe_b_gu, moe_w_down, moe_b_down,
                 ln_ffn_g[i], ln_ffn_b[i], i)
    return h.reshape(batch, seq_len, c)
```

```python
import functools
import math

import jax
import jax.numpy as jnp
from jax import lax
from jax.experimental import pallas as pl
from jax.experimental.pallas import tpu as pltpu

D_MODEL = 1024
HEAD_DIM = 64
N_HEADS = D_MODEL // HEAD_DIM
RW_GN_EPS = 64e-5
IDX_HEADS = 8
IDX_DIM = 64
TOPK_MAX = 256
ROPE_THETA = 10000.0
N_EXPERTS = 32
TOP_K = 4
SWIGLU_LIMIT = 7.0
SWIGLU_ALPHA = 1.702
LN_EPS = 1e-5
DEPTH = 2
DEEPNORM_ALPHA = (2.0 * DEPTH) ** 0.25
IDX_W_SCALE = IDX_HEADS ** -0.5 * IDX_DIM ** -0.5

V7X_LANES = 128
V7X_SUBLANES = 8
V7X_VMEM_LIMIT_BYTES = 56 * 1024 * 1024

SCAN_CHUNK = 64
SCAN_SUB = 2
ROW_TILE = 256
Q_TILE = 128
KEY_GROUP = 256
MOE_TILE = 512
MOE_ROUTER_TILE = 1024
DSA_HEAD_GROUP = 4

F32 = jnp.float32
BF16 = jnp.bfloat16
INT_MIN = -2147483648


def _bf(x):
    return x.astype(BF16)


def _split2(x):
    hi = x.astype(BF16)
    lo = (x - hi.astype(F32)).astype(BF16)
    return hi, lo


def _dot(a, b):
    return jnp.dot(a, b, preferred_element_type=F32)


def _dot_nt(a, b):
    return lax.dot_general(a, b, (((1,), (1,)), ((), ())), preferred_element_type=F32)


def _dot_tn(a, b):
    return lax.dot_general(a, b, (((0,), (0,)), ((), ())), preferred_element_type=F32)


def _layer_norm(h, g, b):
    mu = jnp.mean(h, axis=-1, keepdims=True)
    d = h - mu
    var = jnp.mean(d * d, axis=-1, keepdims=True)
    return d * lax.rsqrt(var + LN_EPS) * g + b


def _sigmoid(z):
    return 1.0 / (1.0 + jnp.exp(-z))


def _params(*sem):
    return pltpu.CompilerParams(dimension_semantics=sem, vmem_limit_bytes=V7X_VMEM_LIMIT_BYTES)


def _full(shape):
    n = len(shape)
    return pl.BlockSpec(shape, lambda *_: (0,) * n)


def _rwkv_pre_kernel(x_ref, xp_ref, mu_ref, wrkv_ref, w1_ref, w2_ref, a1_ref, a2_ref, g1_ref, g2_ref,
                     vec_ref, bd_ref, r_ref, lw_ref, k_ref, v_ref, na_ref, b_ref, g_ref, *, tiles_per_seq):
    i = pl.program_id(0)
    x = x_ref[...]
    prev = xp_ref[V7X_SUBLANES - 1:V7X_SUBLANES, :]
    prev = jnp.where(i % tiles_per_seq == 0, 0.0, prev)
    rows = lax.broadcasted_iota(jnp.int32, x.shape, 0)
    xprev = jnp.where(rows == 0, prev, pltpu.roll(x, 1, 0))
    xx = xprev - x

    def mix(p):
        return x + xx * mu_ref[p:p + 1, :]

    w0 = vec_ref[0:1, :]
    a0 = vec_ref[1:2, :]
    k_k = vec_ref[2:3, :]
    k_a = vec_ref[3:4, :]

    r = _dot(_bf(mix(0)), wrkv_ref[0])
    k = _dot(_bf(mix(1)), wrkv_ref[1])
    v = _dot(_bf(mix(2)), wrkv_ref[2])

    zw = w0 + _dot(_bf(jnp.tanh(_dot(_bf(mix(3)), w1_ref[...]))), w2_ref[...])
    nz = -zw
    softplus = jnp.maximum(nz, 0.0) + jnp.log1p(jnp.exp(-jnp.abs(nz)))
    w_log = -softplus - 0.5
    lw = -jnp.exp(w_log)

    a = _sigmoid(a0 + _dot(_bf(_dot(_bf(mix(4)), a1_ref[...])), a2_ref[...]))
    g = _dot(_bf(_sigmoid(_dot(_bf(mix(5)), g1_ref[...]))), g2_ref[...])

    kk = k * k_k
    sq_hi, sq_lo = _split2(kk * kk)
    gw = bd_ref.shape[0]
    ss = jnp.concatenate([_dot(sq_hi[:, c0:c0 + gw], bd_ref[...]) + _dot(sq_lo[:, c0:c0 + gw], bd_ref[...])
                          for c0 in range(0, kk.shape[1], gw)], axis=1)
    kk = kk / jnp.maximum(jnp.sqrt(ss), 1e-12)
    k = k * (1.0 + (a - 1.0) * k_a)

    r_ref[...] = r
    lw_ref[...] = lw
    k_ref[...] = k
    v_ref[...] = v
    na_ref[...] = -kk
    b_ref[...] = kk * a
    g_ref[...] = g


def _rwkv_pre(x2d, seq_len, mu, w_rkv, w0, w1, w2, a0, a1, a2, g1, g2, k_k, k_a, bd):
    n_tok, c = x2d.shape
    tm = min(ROW_TILE, seq_len)
    assert seq_len % tm == 0 and tm % V7X_SUBLANES == 0
    vecs = jnp.zeros((V7X_SUBLANES, c), F32).at[0].set(w0).at[1].set(a0).at[2].set(k_k).at[3].set(k_a)
    mu8 = jnp.zeros((V7X_SUBLANES, c), F32).at[:6].set(mu)
    row = pl.BlockSpec((tm, c), lambda i: (i, 0))
    sub_per_tile = tm // V7X_SUBLANES
    prev = pl.BlockSpec((V7X_SUBLANES, c), lambda i: (jnp.maximum(i * sub_per_tile - 1, 0), 0))
    ws = [_bf(w_rkv), _bf(w1), _bf(w2), _bf(a1), _bf(a2), _bf(g1), _bf(g2)]
    out = jax.ShapeDtypeStruct((n_tok, c), F32)
    return pl.pallas_call(
        functools.partial(_rwkv_pre_kernel, tiles_per_seq=seq_len // tm),
        grid=(n_tok // tm,),
        in_specs=[row, prev, _full(mu8.shape)] + [_full(w.shape) for w in ws] + [_full(vecs.shape), _full(bd.shape)],
        out_specs=[row] * 7,
        out_shape=[out] * 7,
        compiler_params=_params("parallel"),
        name="rwkv_pre",
    )(x2d, x2d, mu8, *ws, vecs, bd)


def _rwkv_scan_kernel(r_ref, lw_ref, k_ref, v_ref, na_ref, b_ref, g_ref, vec_ref, o_ref, s_ref):
    R, C = r_ref.shape
    L = SCAN_CHUNK
    N = HEAD_DIM
    PW = 2 * N
    assert PW == V7X_LANES and 2 * L == PW and R % L == 0
    subs = range(R // L)
    rs = [slice(c * L, (c + 1) * L) for c in subs]

    @pl.when(pl.program_id(1) == 0)
    def _():
        s_ref[...] = jnp.zeros_like(s_ref)

    def head_a(width, n_rows=L):
        return (lax.broadcasted_iota(jnp.int32, (n_rows, width), 1) & N) == 0

    def split_ab(z):
        m = head_a(z.shape[1])
        return jnp.concatenate([jnp.where(m, z, 0.0), jnp.where(m, 0.0, z)], axis=0)

    def split_ba(z):
        m = head_a(z.shape[1])
        return jnp.concatenate([jnp.where(m, 0.0, z), jnp.where(m, z, 0.0)], axis=0)

    rows = lax.broadcasted_iota(jnp.int32, (R, R), 0)
    cols = lax.broadcasted_iota(jnp.int32, (R, R), 1)
    lw = lw_ref[...]
    tri = jnp.where(jnp.logical_and(rows >= cols, (rows ^ cols) < L), 1.0, 0.0).astype(BF16)
    h1 = lw.astype(BF16)
    r1 = lw - h1.astype(F32)
    h2 = r1.astype(BF16)
    h3 = (r1 - h2.astype(F32)).astype(BF16)
    cs = _dot(tri, h1) + _dot(tri, h2) + _dot(tri, h3)
    cs_end = [cs[(c + 1) * L - 1:(c + 1) * L, :] for c in subs]
    g_incl = jnp.exp(cs)
    g_excl = jnp.exp(cs - lw)
    g_inv = jnp.exp(-cs)
    g_tail = jnp.exp(jnp.concatenate([jnp.broadcast_to(e, (L, C)) for e in cs_end], axis=0) - cs)
    g_last = [jnp.exp(e) for e in cs_end]

    r = r_ref[...]
    k = k_ref[...]
    v = v_ref[...]
    b = b_ref[...]
    at = na_ref[...] * g_excl
    rt = r * g_incl
    bt = b * g_inv
    kt = k * g_inv
    bh = b * g_tail
    kh = k * g_tail

    n_pairs = C // PW
    units = range(len(subs) * n_pairs)
    sls = [(rs[i // n_pairs], slice((i % n_pairs) * PW, (i % n_pairs + 1) * PW)) for i in units]
    pairs = units
    t_idx = lax.broadcasted_iota(jnp.int32, (L, PW), 0)
    s_idx = lax.broadcasted_iota(jnp.int32, (L, PW), 1) & (N - 1)
    strict2 = t_idx > s_idx
    incl2 = t_idx >= s_idx
    is_a = head_a(PW)
    is_a2 = head_a(PW, 2 * L)

    lhs = [jnp.concatenate([at[sl], rt[sl]], axis=0) for sl in sls]
    g_a = [_dot_nt(_bf(jnp.where(is_a2, lhs[j], 0.0)), _bf(jnp.concatenate([bt[sls[j]], kt[sls[j]]], axis=0)))
           for j in pairs]
    g_b = [_dot_nt(_bf(jnp.where(is_a2, 0.0, lhs[j])), _bf(jnp.concatenate([kt[sls[j]], bt[sls[j]]], axis=0)))
           for j in pairs]
    pp = [jnp.where(strict2, jnp.where(is_a, g_a[j][:L], g_b[j][:L]), 0.0) for j in pairs]
    m_ak = [jnp.where(strict2, jnp.where(is_a, g_b[j][:L], g_a[j][:L]), 0.0) for j in pairs]
    w_rb = [jnp.where(incl2, jnp.where(is_a, g_a[j][L:], g_b[j][L:]), 0.0) for j in pairs]
    w_rk = [jnp.where(incl2, jnp.where(is_a, g_b[j][L:], g_a[j][L:]), 0.0) for j in pairs]
    mv = [_dot(_bf(m_ak[j]), _bf(split_ba(v[sls[j]]))) for j in pairs]
    xs = [jnp.concatenate([at[sls[j]], mv[j]], axis=1) for j in pairs]
    levels = int(math.log2(L))
    for lvl in range(levels):
        if lvl < levels - 1:
            z = [_dot(_bf(pp[j]), _bf(split_ab(jnp.concatenate([xs[j], pp[j]], axis=1)))) for j in pairs]
            xs = [xs[j] + z[j][:, :2 * PW] for j in pairs]
            pp = [z[j][:, 2 * PW:] for j in pairs]
        else:
            xs = [xs[j] + _dot(_bf(pp[j]), _bf(split_ab(xs[j]))) for j in pairs]
    zy = [_dot(_bf(jnp.concatenate([w_rb[j], w_rk[j]], axis=1)),
               _bf(jnp.concatenate([split_ab(xs[j]),
                                    split_ba(jnp.concatenate([jnp.zeros((L, PW), F32), v[sls[j]]], axis=1))], axis=0)))
          for j in pairs]
    vr = lax.broadcasted_iota(jnp.int32, (PW, PW), 0)
    kc = lax.broadcasted_iota(jnp.int32, (PW, PW), 1)
    same_head = (vr & N) == (kc & N)
    state = [s_ref[p] for p in range(n_pairs)]
    y_rows = []
    for c in subs:
        ids = [c * n_pairs + p for p in range(n_pairs)]
        uy = [_dot_nt(_bf(jnp.concatenate([xs[j][:, :PW], rt[sls[j]] + zy[j][:, :PW]], axis=0)), _bf(state[j % n_pairs]))
              for j in ids]
        u = [uy[p][:L] + xs[j][:, PW:] for p, j in enumerate(ids)]
        y_rows.append(jnp.concatenate([uy[p][L:] + zy[j][:, PW:] for p, j in enumerate(ids)], axis=1))
        upd = [_dot_tn(_bf(jnp.concatenate([u[p], v[sls[j]]], axis=0)),
                       _bf(jnp.concatenate([bh[sls[j]], kh[sls[j]]], axis=0))) for p, j in enumerate(ids)]
        state = [state[p] * g_last[c][:, sls[j][1]] + jnp.where(same_head, upd[p], 0.0) for p, j in enumerate(ids)]
    for p in range(n_pairs):
        s_ref[p] = state[p]

    yy = jnp.concatenate(y_rows, axis=0)
    gw = 2 * PW
    jr = lax.broadcasted_iota(jnp.int32, (gw, gw), 0)
    jc = lax.broadcasted_iota(jnp.int32, (gw, gw), 1)
    ones_bd = jnp.where((jr ^ jc) < N, 1.0, 0.0).astype(BF16)

    def head_sum(z):
        hi, lo = _split2(z)
        return jnp.concatenate([_dot(hi[:, c0:c0 + gw], ones_bd) + _dot(lo[:, c0:c0 + gw], ones_bd)
                                for c0 in range(0, C, gw)], axis=1)

    yd = yy - head_sum(yy) * (1.0 / N)
    yv = head_sum(yd * yd) * (1.0 / N)
    yn = yd * lax.rsqrt(yv + RW_GN_EPS) * vec_ref[0:1, :] + vec_ref[1:2, :]
    bonus = head_sum(r * k * vec_ref[2:3, :]) * v
    o_ref[...] = (yn + bonus) * g_ref[...]


def _rwkv_scan(r, lw, k, v, na, b, g, lnx_g, lnx_b, r_k, batch, seq_len):
    n_tok, c = r.shape
    rows = SCAN_CHUNK * SCAN_SUB
    assert seq_len % rows == 0
    vecs = jnp.zeros((V7X_SUBLANES, c), F32).at[0].set(lnx_g).at[1].set(lnx_b).at[2].set(r_k.reshape(c))
    chunks = seq_len // rows
    blk = pl.BlockSpec((rows, c), lambda bi, ci: (bi * chunks + ci, 0))
    return pl.pallas_call(
        _rwkv_scan_kernel,
        grid=(batch, chunks),
        in_specs=[blk] * 7 + [_full(vecs.shape)],
        out_specs=blk,
        out_shape=jax.ShapeDtypeStruct((n_tok, c), F32),
        scratch_shapes=[pltpu.VMEM((c // V7X_LANES, V7X_LANES, V7X_LANES), F32)],
        compiler_params=_params("parallel", "arbitrary"),
        name="rwkv_scan",
    )(r, lw, k, v, na, b, g, vecs)


def _proj_ln_kernel(x_ref, z_ref, w_ref, gb_ref, o_ref):
    m = _dot(_bf(z_ref[...]), w_ref[...])
    o_ref[...] = _layer_norm(DEEPNORM_ALPHA * x_ref[...] + m, gb_ref[0:1, :], gb_ref[1:2, :])


def _proj_ln(x2d, z2d, w, ln_g, ln_b):
    n_tok, c = x2d.shape
    tm = min(2 * ROW_TILE, n_tok)
    gb = jnp.zeros((V7X_SUBLANES, c), F32).at[0].set(ln_g).at[1].set(ln_b)
    row = pl.BlockSpec((tm, c), lambda i: (i, 0))
    return pl.pallas_call(
        _proj_ln_kernel,
        grid=(n_tok // tm,),
        in_specs=[row, row, _full(w.shape), _full(gb.shape)],
        out_specs=row,
        out_shape=jax.ShapeDtypeStruct((n_tok, c), F32),
        compiler_params=_params("parallel"),
        name="proj_ln",
    )(x2d, z2d, _bf(w), gb)


def _router_kernel(x_ref, wt_ref, b_ref, earlier_ref, e_ref, gate_ref, rank_ref, cnt_ref):
    x_hi, x_lo = _split2(x_ref[...])
    w_hi, w_lo = _split2(wt_ref[...])
    logits = _dot_nt(w_hi, x_hi) + _dot_nt(w_hi, x_lo) + _dot_nt(w_lo, x_hi) + b_ref[...]
    n_e = logits.shape[0]
    eid = lax.broadcasted_iota(jnp.int32, logits.shape, 0)
    tops, idxs, hots = [], [], []
    for _ in range(TOP_K):
        m = jnp.max(logits, axis=0, keepdims=True)
        idx = jnp.min(jnp.where(logits == m, eid, n_e), axis=0, keepdims=True)
        tops.append(m)
        idxs.append(idx)
        hots.append(eid == idx)
        logits = jnp.where(eid == idx, -jnp.inf, logits)
    ex = [jnp.exp(t - tops[0]) for t in tops]
    denom = ex[0] + ex[1] + ex[2] + ex[3]
    e_ref[...] = jnp.concatenate(idxs, axis=0)
    gate_ref[...] = jnp.concatenate([e / denom for e in ex], axis=0)
    chosen = sum(jnp.where(hot, 1.0, 0.0) for hot in hots)
    before = _dot(_bf(chosen), earlier_ref[...])
    rank_ref[...] = jnp.concatenate([jnp.sum(jnp.where(hot, before, 0.0), axis=0, keepdims=True) for hot in hots],
                                    axis=0).astype(jnp.int32)
    cnt_ref[...] = jnp.broadcast_to(jnp.sum(chosen, axis=1, keepdims=True), cnt_ref.shape).astype(jnp.int32)


def _router(x2d, w_router, b_router):
    n_tok, c = x2d.shape
    tm = min(MOE_ROUTER_TILE, n_tok)
    wt = w_router.T
    bcol = b_router.reshape(N_EXPERTS, 1)
    tok = jnp.arange(tm, dtype=jnp.int32)
    earlier = (tok[:, None] < tok[None, :]).astype(BF16)
    tile_of = lambda i: (0, i)
    return pl.pallas_call(
        _router_kernel,
        grid=(n_tok // tm,),
        in_specs=[pl.BlockSpec((tm, c), lambda i: (i, 0)), _full(wt.shape), _full(bcol.shape), _full(earlier.shape)],
        out_specs=[pl.BlockSpec((TOP_K, tm), tile_of)] * 3 + [pl.BlockSpec((None, N_EXPERTS, V7X_LANES), lambda i: (i, 0, 0))],
        out_shape=[jax.ShapeDtypeStruct((TOP_K, n_tok), jnp.int32), jax.ShapeDtypeStruct((TOP_K, n_tok), F32),
                   jax.ShapeDtypeStruct((TOP_K, n_tok), jnp.int32),
                   jax.ShapeDtypeStruct((n_tok // tm, N_EXPERTS, V7X_LANES), jnp.int32)],
        compiler_params=_params("parallel"),
        name="moe_router",
    )(x2d, wt, bcol, earlier)


def _expert_kernel(blk_e_ref, n_used_ref, x_ref, wgu_ref, bgu_ref, wd_ref, bd_ref, y_ref, wgu_bf, wd_bf):
    i = pl.program_id(0)
    e = blk_e_ref[i]
    changed = jnp.logical_or(i == 0, e != blk_e_ref[jnp.maximum(i - 1, 0)])

    @pl.when(changed)
    def _():
        wgu_bf[...] = _bf(wgu_ref[...])
        wd_bf[...] = _bf(wd_ref[...])

    @pl.when(i < n_used_ref[0])
    def _():
        f = wd_ref.shape[0]
        gu = _dot(_bf(x_ref[...]), wgu_bf[...]) + bgu_ref[...]
        glu = jnp.minimum(gu[:, :f], SWIGLU_LIMIT)
        lin = jnp.clip(gu[:, f:], -SWIGLU_LIMIT, SWIGLU_LIMIT)
        act = glu * _sigmoid(SWIGLU_ALPHA * glu) * (lin + 1.0)
        y_ref[...] = _dot(_bf(act), wd_bf[...]) + bd_ref[...]

    @pl.when(i >= n_used_ref[0])
    def _():
        y_ref[...] = jnp.zeros_like(y_ref)


def _experts(x_rows, blk_e, n_used, w_gu, b_gu, w_down, b_down, layer):
    n_rows, c = x_rows.shape
    n_l, n_e, _, f2 = w_gu.shape
    f = f2 // 2
    tb = MOE_TILE
    grid_spec = pltpu.PrefetchScalarGridSpec(
        num_scalar_prefetch=2,
        grid=(n_rows // tb,),
        in_specs=[
            pl.BlockSpec((tb, c), lambda i, be, nu: (jnp.minimum(i, nu[0] - 1), 0)),
            pl.BlockSpec((None, None, c, f2), lambda i, be, nu: (layer, be[i], 0, 0)),
            pl.BlockSpec((None, None, 1, f2), lambda i, be, nu: (layer, be[i], 0, 0)),
            pl.BlockSpec((None, None, f, c), lambda i, be, nu: (layer, be[i], 0, 0)),
            pl.BlockSpec((None, None, 1, c), lambda i, be, nu: (layer, be[i], 0, 0)),
        ],
        out_specs=pl.BlockSpec((tb, c), lambda i, be, nu: (i, 0)),
        scratch_shapes=[pltpu.VMEM((c, f2), BF16), pltpu.VMEM((f, c), BF16)],
    )
    return pl.pallas_call(
        _expert_kernel,
        grid_spec=grid_spec,
        out_shape=jax.ShapeDtypeStruct((n_rows, c), F32),
        compiler_params=_params("arbitrary"),
        name="moe_experts",
    )(blk_e, n_used, x_rows, w_gu, b_gu.reshape(n_l, n_e, 1, f2), w_down, b_down.reshape(n_l, n_e, 1, c))


def _combine_ln_kernel(x_ref, y_ref, gate_ref, gb_ref, o_ref):
    acc = DEEPNORM_ALPHA * x_ref[...]
    for j in range(TOP_K):
        acc = acc + gate_ref[:, j:j + 1] * y_ref[j]
    o_ref[...] = _layer_norm(acc, gb_ref[0:1, :], gb_ref[1:2, :])


def _combine_ln(x2d, y4, gates, ln_g, ln_b):
    n_tok, c = x2d.shape
    tm = min(ROW_TILE, n_tok)
    gb = jnp.zeros((V7X_SUBLANES, c), F32).at[0].set(ln_g).at[1].set(ln_b)
    return pl.pallas_call(
        _combine_ln_kernel,
        grid=(n_tok // tm,),
        in_specs=[pl.BlockSpec((tm, c), lambda i: (i, 0)), pl.BlockSpec((TOP_K, tm, c), lambda i: (0, i, 0)),
                  pl.BlockSpec((tm, TOP_K), lambda i: (i, 0)), _full(gb.shape)],
        out_specs=pl.BlockSpec((tm, c), lambda i: (i, 0)),
        out_shape=jax.ShapeDtypeStruct((n_tok, c), F32),
        compiler_params=_params("parallel"),
        name="moe_combine_ln",
    )(x2d, y4, gates, gb)


def _moe(x2d, w_router, b_router, w_gu, b_gu, w_down, b_down, ln_g, ln_b, layer):
    n_tok, c = x2d.shape
    top_e, gates, rank, tile_cnt = _router(x2d, w_router, b_router)
    tb = MOE_TILE
    tile_cnt = tile_cnt[:, :, 0]
    n_tiles = tile_cnt.shape[0]
    counts = jnp.sum(tile_cnt, axis=0)
    padded = (counts + tb - 1) // tb * tb
    pad_end = jnp.cumsum(padded)
    pad_start = pad_end - padded
    base = pad_start[None, :] + jnp.cumsum(tile_cnt, axis=0) - tile_cnt
    base_tok = jnp.broadcast_to(base[:, None, :], (n_tiles, n_tok // n_tiles, N_EXPERTS)).reshape(n_tok, N_EXPERTS)
    hot = top_e[:, :, None] == jnp.arange(N_EXPERTS, dtype=jnp.int32)
    dest = jnp.sum(jnp.where(hot, base_tok[None], 0), axis=-1) + rank
    n_rows = -(-(n_tok * TOP_K + N_EXPERTS * (tb - 1)) // tb) * tb
    blk_start = jnp.arange(n_rows // tb, dtype=jnp.int32) * tb
    blk_e = jnp.minimum(jnp.sum((pad_end[None, :] <= blk_start[:, None]).astype(jnp.int32), axis=1), N_EXPERTS - 1)
    n_used = (pad_end[-1] // tb).astype(jnp.int32).reshape(1)
    tok = jnp.broadcast_to(jnp.arange(n_tok, dtype=jnp.int32)[None, :], (TOP_K, n_tok))
    rows_tok = jnp.zeros((n_rows,), jnp.int32).at[dest.reshape(-1)].set(
        tok.reshape(-1), unique_indices=True, mode="promise_in_bounds")
    x_rows = x2d.at[rows_tok].get(mode="promise_in_bounds")
    y_rows = _experts(x_rows, blk_e, n_used, w_gu, b_gu, w_down, b_down, layer)
    y4 = y_rows.at[dest.reshape(-1)].get(unique_indices=True, mode="promise_in_bounds").reshape(TOP_K, n_tok, c)
    return _combine_ln(x2d, y4, gates.T, ln_g, ln_b)


def _rope(z, cos, sin_signed):
    half = HEAD_DIM // 2
    width = z.shape[-1]
    lane = lax.broadcasted_iota(jnp.int32, z.shape, z.ndim - 1)
    swapped = jnp.where(lane % HEAD_DIM < half, pltpu.roll(z, width - half, z.ndim - 1), pltpu.roll(z, half, z.ndim - 1))
    return z * cos + swapped * sin_signed


def _dsa_proj_kernel(x_ref, w_ref, cos_ref, sin_ref, gb_ref, q_ref, k_ref, v_ref, qi_ref, ki_ref, wi_ref):
    c = x_ref.shape[1]
    proj = _dot(_bf(x_ref[...]), w_ref[...])
    cos2 = cos_ref[...]
    sin2 = sin_ref[...]
    q_w = q_ref.shape[1]
    qi_w = qi_ref.shape[1]
    q = proj[:, :q_w]
    q_ref[...] = _bf(_rope(q, jnp.tile(cos2, (1, q_w // V7X_LANES)), jnp.tile(sin2, (1, q_w // V7X_LANES))))
    kv = proj[:, q_w:q_w + V7X_LANES]
    k_ref[...] = _bf(_rope(kv, cos2, sin2)[:, :HEAD_DIM])
    v_ref[...] = _bf(kv[:, HEAD_DIM:])
    o = q_w + V7X_LANES
    qi = proj[:, o:o + qi_w]
    qi_ref[...] = _rope(qi, jnp.tile(cos2, (1, qi_w // V7X_LANES)), jnp.tile(sin2, (1, qi_w // V7X_LANES)))
    o = o + qi_w
    kiw = proj[:, o:o + V7X_LANES]
    in_ki = lax.broadcasted_iota(jnp.int32, kiw.shape, 1) < IDX_DIM
    mu = jnp.sum(kiw, axis=-1, keepdims=True) * (1.0 / IDX_DIM)
    d = jnp.where(in_ki, kiw - mu, 0.0)
    var = jnp.sum(d * d, axis=-1, keepdims=True) * (1.0 / IDX_DIM)
    ki = d * lax.rsqrt(var + LN_EPS) * gb_ref[0:1, :] + gb_ref[1:2, :]
    ki_ref[...] = _rope(ki, cos2, sin2)[:, :IDX_DIM]
    o = o + V7X_LANES
    wi_ref[...] = proj[:, o:o + IDX_HEADS] * IDX_W_SCALE


def _dsa_proj(x2d, seq_len, w_in, idx_g, idx_b, cos2, sin2):
    n_tok, c = x2d.shape
    q_w = N_HEADS * HEAD_DIM
    qi_w = IDX_HEADS * IDX_DIM
    o_ki = q_w + 2 * HEAD_DIM + qi_w
    w_pad = jnp.concatenate([w_in[:, :o_ki + IDX_DIM], jnp.zeros((c, V7X_LANES - IDX_DIM), F32),
                             w_in[:, o_ki + IDX_DIM:], jnp.zeros((c, V7X_LANES - IDX_HEADS), F32)], axis=1)
    tm = min(ROW_TILE, seq_len)
    tiles = seq_len // tm
    gb = jnp.zeros((V7X_SUBLANES, V7X_LANES), F32).at[0, :IDX_DIM].set(idx_g).at[1, :IDX_DIM].set(idx_b)
    row = lambda w: pl.BlockSpec((tm, w), lambda i: (i, 0))
    tab = pl.BlockSpec((tm, V7X_LANES), lambda i: (i % tiles, 0))
    shapes = [(q_w, BF16), (HEAD_DIM, BF16), (HEAD_DIM, BF16), (qi_w, F32), (IDX_DIM, F32), (IDX_HEADS, F32)]
    return pl.pallas_call(
        _dsa_proj_kernel,
        grid=(n_tok // tm,),
        in_specs=[row(c), _full(w_pad.shape), tab, tab, _full(gb.shape)],
        out_specs=[row(w) for w, _ in shapes],
        out_shape=[jax.ShapeDtypeStruct((n_tok, w), dt) for w, dt in shapes],
        compiler_params=_params("parallel"),
        name="dsa_proj",
    )(x2d, _bf(w_pad), cos2, sin2, gb)


def _dsa_attn_kernel(x_ref, q_ref, qi_ref, wi_ref, ki_ref, k_ref, v_ref, wo_ref, gb_ref, o_ref, *, q_block0, n_keys, k_sel):
    qb = q_block0 + pl.program_id(1)
    tq = q_ref.shape[0]
    nk = n_keys

    ki_hi, ki_lo = _split2(ki_ref[0:nk, :])
    ki3 = jnp.concatenate([ki_hi, ki_hi, ki_lo], axis=1)
    qi = qi_ref[...]
    wi = wi_ref[...]
    s = jnp.zeros((tq, nk), F32)
    for h in range(IDX_HEADS):
        q_hi, q_lo = _split2(qi[:, h * IDX_DIM:(h + 1) * IDX_DIM])
        sh = _dot_nt(jnp.concatenate([q_hi, q_lo, q_hi], axis=1), ki3)
        s = s + jnp.maximum(sh, 0.0) * wi[:, h:h + 1]

    key_pos = lax.broadcasted_iota(jnp.int32, (tq, nk), 1)
    q_pos = qb * tq + lax.broadcasted_iota(jnp.int32, (tq, nk), 0)
    causal = key_pos <= q_pos

    bits = pltpu.bitcast(s, jnp.int32)
    keys = jnp.where(causal, bits ^ ((bits >> 31) & jnp.int32(0x7FFFFFFF)), jnp.int32(INT_MIN))

    def bit_step(i, tau):
        cand = tau | jnp.left_shift(jnp.int32(1), 31 - i)
        cnt = jnp.sum(jnp.where(keys >= (cand ^ jnp.int32(INT_MIN)), 1.0, 0.0), axis=1, keepdims=True)
        return jnp.where(cnt >= k_sel, cand, tau)

    tau = lax.fori_loop(0, 32, bit_step, jnp.zeros((tq, 1), jnp.int32)) ^ jnp.int32(INT_MIN)
    above = keys > tau
    tied = keys == tau
    room = k_sel - jnp.sum(jnp.where(above, 1.0, 0.0), axis=1, keepdims=True)
    tied_bf = jnp.where(tied, 1.0, 0.0).astype(BF16)
    cr = lax.broadcasted_iota(jnp.int32, (V7X_LANES, V7X_LANES), 0)
    cc = lax.broadcasted_iota(jnp.int32, (V7X_LANES, V7X_LANES), 1)
    earlier_bf = jnp.where(cr < cc, 1.0, 0.0).astype(BF16)
    carry = jnp.zeros((tq, 1), F32)
    before = []
    for ch in range(nk // V7X_LANES):
        t = tied_bf[:, ch * V7X_LANES:(ch + 1) * V7X_LANES]
        before.append(_dot(t, earlier_bf) + carry)
        carry = carry + jnp.sum(t.astype(F32), axis=1, keepdims=True)
    before = jnp.concatenate(before, axis=1)
    sel = jnp.logical_and(jnp.logical_or(above, jnp.logical_and(tied, before < room)), causal)
    bias = jnp.where(sel, 0.0, -1e30)

    k = k_ref[0:nk, :]
    v = v_ref[0:nk, :]
    q = q_ref[...] * (HEAD_DIM ** -0.5)
    n_heads = q.shape[1] // HEAD_DIM
    outs = []
    for g0 in range(0, n_heads, DSA_HEAD_GROUP):
        hs = range(g0, g0 + DSA_HEAD_GROUP)
        logits = _dot_nt(jnp.concatenate([q[:, h * HEAD_DIM:(h + 1) * HEAD_DIM] for h in hs], axis=0), k)
        ps, inv = [], []
        for i in range(DSA_HEAD_GROUP):
            lg = logits[i * tq:(i + 1) * tq, :] + bias
            p = jnp.exp(lg - jnp.max(lg, axis=1, keepdims=True))
            inv.append(1.0 / jnp.sum(p, axis=1, keepdims=True))
            ps.append(_bf(p))
        o = _dot(jnp.concatenate(ps, axis=0), v)
        outs += [o[i * tq:(i + 1) * tq, :] * inv[i] for i in range(DSA_HEAD_GROUP)]
    att = jnp.concatenate(outs, axis=1)
    m_out = _dot(_bf(att), wo_ref[...])
    o_ref[...] = _layer_norm(DEEPNORM_ALPHA * x_ref[...] + m_out, gb_ref[0:1, :], gb_ref[1:2, :])


def _dsa_attn(x2d, batch, seq_len, q, k, v, qi, ki, wi, w_o, ln_g, ln_b):
    n_tok, c = x2d.shape
    tq = Q_TILE
    k_sel = min(TOPK_MAX, seq_len // 4)
    kg = min(KEY_GROUP, seq_len)
    assert seq_len % kg == 0 and kg % tq == 0
    qb_per_group = kg // tq
    qb_per_seq = seq_len // tq
    ki3 = ki.reshape(batch, seq_len, IDX_DIM)
    k3 = k.reshape(batch, seq_len, HEAD_DIM)
    v3 = v.reshape(batch, seq_len, HEAD_DIM)
    gb = jnp.zeros((V7X_SUBLANES, c), F32).at[0].set(ln_g).at[1].set(ln_b)
    wo = _bf(w_o)
    outs = []
    for grp in range(seq_len // kg):
        q0 = grp * qb_per_group
        n_keys = (grp + 1) * kg
        row = lambda w: pl.BlockSpec((tq, w), lambda bi, j: (bi * qb_per_seq + q0 + j, 0))
        per_batch = lambda w: pl.BlockSpec((None, seq_len, w), lambda bi, j: (bi, 0, 0))
        out = pl.pallas_call(
            functools.partial(_dsa_attn_kernel, q_block0=q0, n_keys=n_keys, k_sel=k_sel),
            grid=(batch, qb_per_group),
            in_specs=[row(c), row(q.shape[1]), row(qi.shape[1]), row(wi.shape[1]),
                      per_batch(IDX_DIM), per_batch(HEAD_DIM), per_batch(HEAD_DIM),
                      _full(wo.shape), _full(gb.shape)],
            out_specs=pl.BlockSpec((None, tq, c), lambda bi, j: (bi, j, 0)),
            out_shape=jax.ShapeDtypeStruct((batch, kg, c), F32),
            compiler_params=_params("parallel", "parallel"),
            name=f"dsa_attn_{n_keys}",
        )(x2d, q, qi, wi, ki3, k3, v3, wo, gb)
        outs.append(out)
    return jnp.concatenate(outs, axis=1).reshape(n_tok, c)


def _rope_tables(seq_len):
    pos = jnp.arange(seq_len, dtype=F32)
    inv = 1.0 / (ROPE_THETA ** (jnp.arange(0, HEAD_DIM, 2, dtype=F32) / HEAD_DIM))
    ang = pos[:, None] * inv[None, :]
    cos, sin = jnp.cos(ang), jnp.sin(ang)
    cos2 = jnp.tile(jnp.concatenate([cos, cos], axis=1), (1, V7X_LANES // HEAD_DIM))
    sin2 = jnp.tile(jnp.concatenate([-sin, sin], axis=1), (1, V7X_LANES // HEAD_DIM))
    return cos2, sin2


def kernel(x, rw_mu, rw_w_rkv, rw_w0, rw_w1, rw_w2, rw_a0, rw_a1, rw_a2, rw_g1, rw_g2, rw_k_k, rw_k_a, rw_r_k, rw_lnx_g, rw_lnx_b, rw_w_o, ds_w_in, ds_idx_ln_g, ds_idx_ln_b, ds_w_o, ln_mix_g, ln_mix_b, moe_w_router, moe_b_router, moe_w_gu, moe_b_gu, moe_w_down, moe_b_down, ln_ffn_g, ln_ffn_b):
    batch, seq_len, c = x.shape
    assert c == D_MODEL
    h = x.reshape(batch * seq_len, c)
    cos2, sin2 = _rope_tables(seq_len)
    head_of = jnp.arange(2 * V7X_LANES, dtype=jnp.int32) // HEAD_DIM
    bd = (head_of[:, None] == head_of[None, :]).astype(BF16)
    n_mixers = 2
    for i in range(DEPTH):
        j = i // n_mixers
        if i % n_mixers == 0:
            r, lw, k, v, na, b, g = _rwkv_pre(h, seq_len, rw_mu[j], rw_w_rkv[j], rw_w0[j], rw_w1[j], rw_w2[j],
                                              rw_a0[j], rw_a1[j], rw_a2[j], rw_g1[j], rw_g2[j], rw_k_k[j], rw_k_a[j], bd)
            z = _rwkv_scan(r, lw, k, v, na, b, g, rw_lnx_g[j], rw_lnx_b[j], rw_r_k[j], batch, seq_len)
            h = _proj_ln(h, z, rw_w_o[j], ln_mix_g[i], ln_mix_b[i])
        else:
            q, k, v, qi, ki, wi = _dsa_proj(h, seq_len, ds_w_in[j], ds_idx_ln_g[j], ds_idx_ln_b[j], cos2, sin2)
            h = _dsa_attn(h, batch, seq_len, q, k, v, qi, ki, wi, ds_w_o[j], ln_mix_g[i], ln_mix_b[i])
        h = _moe(h, moe_w_router[i], moe_b_router[i], moe_w_gu, moe_b_gu, moe_w_down, moe_b_down,
                 ln_ffn_g[i], ln_ffn_b[i], i)
    return h.reshape(batch, seq_len, c)
```

```python
import functools
import math

import jax
import jax.numpy as jnp
from jax import lax
from jax.experimental import pallas as pl
from jax.experimental.pallas import tpu as pltpu

D_MODEL = 1024
HEAD_DIM = 64
N_HEADS = D_MODEL // HEAD_DIM
RW_GN_EPS = 64e-5
IDX_HEADS = 8
IDX_DIM = 64
TOPK_MAX = 256
ROPE_THETA = 10000.0
N_EXPERTS = 32
TOP_K = 4
SWIGLU_LIMIT = 7.0
SWIGLU_ALPHA = 1.702
LN_EPS = 1e-5
DEPTH = 2
DEEPNORM_ALPHA = (2.0 * DEPTH) ** 0.25
IDX_W_SCALE = IDX_HEADS ** -0.5 * IDX_DIM ** -0.5

V7X_LANES = 128
V7X_SUBLANES = 8
V7X_VMEM_LIMIT_BYTES = 56 * 1024 * 1024

SCAN_CHUNK = 64
SCAN_SUB = 2
ROW_TILE = 256
Q_TILE = 128
KEY_GROUP = 256
MOE_TILE = 512
MOE_ROUTER_TILE = 1024
DSA_HEAD_GROUP = 4
DSA_TWO_BIT_SEARCH_KEYS = 768
DSA_UNROLLED_SEARCH_KEYS = 512

F32 = jnp.float32
BF16 = jnp.bfloat16
INT_MIN = -2147483648


def _bf(x):
    return x.astype(BF16)


def _split2(x):
    hi = x.astype(BF16)
    lo = (x - hi.astype(F32)).astype(BF16)
    return hi, lo


def _dot(a, b):
    return jnp.dot(a, b, preferred_element_type=F32)


def _dot_nt(a, b):
    return lax.dot_general(a, b, (((1,), (1,)), ((), ())), preferred_element_type=F32)


def _dot_tn(a, b):
    return lax.dot_general(a, b, (((0,), (0,)), ((), ())), preferred_element_type=F32)


def _layer_norm(h, g, b):
    mu = jnp.mean(h, axis=-1, keepdims=True)
    d = h - mu
    var = jnp.mean(d * d, axis=-1, keepdims=True)
    return d * lax.rsqrt(var + LN_EPS) * g + b


def _sigmoid(z):
    return 1.0 / (1.0 + jnp.exp(-z))


def _params(*sem):
    return pltpu.CompilerParams(dimension_semantics=sem, vmem_limit_bytes=V7X_VMEM_LIMIT_BYTES)


def _full(shape):
    n = len(shape)
    return pl.BlockSpec(shape, lambda *_: (0,) * n)


def _rwkv_pre_kernel(x_ref, xp_ref, mu_ref, wrkv_ref, w1_ref, w2_ref, a1_ref, a2_ref, g1_ref, g2_ref,
                     vec_ref, bd_ref, r_ref, lw_ref, k_ref, v_ref, na_ref, b_ref, g_ref, *, tiles_per_seq):
    i = pl.program_id(0)
    x = x_ref[...]
    prev = xp_ref[V7X_SUBLANES - 1:V7X_SUBLANES, :]
    prev = jnp.where(i % tiles_per_seq == 0, 0.0, prev)
    rows = lax.broadcasted_iota(jnp.int32, x.shape, 0)
    xprev = jnp.where(rows == 0, prev, pltpu.roll(x, 1, 0))
    xx = xprev - x

    def mix(p):
        return x + xx * mu_ref[p:p + 1, :]

    w0 = vec_ref[0:1, :]
    a0 = vec_ref[1:2, :]
    k_k = vec_ref[2:3, :]
    k_a = vec_ref[3:4, :]

    r = _dot(_bf(mix(0)), wrkv_ref[0])
    k = _dot(_bf(mix(1)), wrkv_ref[1])
    v = _dot(_bf(mix(2)), wrkv_ref[2])

    zw = w0 + _dot(_bf(jnp.tanh(_dot(_bf(mix(3)), w1_ref[...]))), w2_ref[...])
    nz = -zw
    softplus = jnp.maximum(nz, 0.0) + jnp.log1p(jnp.exp(-jnp.abs(nz)))
    w_log = -softplus - 0.5
    lw = -jnp.exp(w_log)

    a = _sigmoid(a0 + _dot(_bf(_dot(_bf(mix(4)), a1_ref[...])), a2_ref[...]))
    g = _dot(_bf(_sigmoid(_dot(_bf(mix(5)), g1_ref[...]))), g2_ref[...])

    kk = k * k_k
    sq_hi, sq_lo = _split2(kk * kk)
    gw = bd_ref.shape[0]
    ss = jnp.concatenate([_dot(sq_hi[:, c0:c0 + gw], bd_ref[...]) + _dot(sq_lo[:, c0:c0 + gw], bd_ref[...])
                          for c0 in range(0, kk.shape[1], gw)], axis=1)
    kk = kk / jnp.maximum(jnp.sqrt(ss), 1e-12)
    k = k * (1.0 + (a - 1.0) * k_a)

    r_ref[...] = r
    lw_ref[...] = lw
    k_ref[...] = k
    v_ref[...] = v
    na_ref[...] = -kk
    b_ref[...] = kk * a
    g_ref[...] = g


def _rwkv_pre(x2d, seq_len, mu, w_rkv, w0, w1, w2, a0, a1, a2, g1, g2, k_k, k_a, bd):
    n_tok, c = x2d.shape
    tm = min(ROW_TILE, seq_len)
    assert seq_len % tm == 0 and tm % V7X_SUBLANES == 0
    vecs = jnp.zeros((V7X_SUBLANES, c), F32).at[0].set(w0).at[1].set(a0).at[2].set(k_k).at[3].set(k_a)
    mu8 = jnp.zeros((V7X_SUBLANES, c), F32).at[:6].set(mu)
    row = pl.BlockSpec((tm, c), lambda i: (i, 0))
    sub_per_tile = tm // V7X_SUBLANES
    prev = pl.BlockSpec((V7X_SUBLANES, c), lambda i: (jnp.maximum(i * sub_per_tile - 1, 0), 0))
    ws = [_bf(w_rkv), _bf(w1), _bf(w2), _bf(a1), _bf(a2), _bf(g1), _bf(g2)]
    out = jax.ShapeDtypeStruct((n_tok, c), F32)
    return pl.pallas_call(
        functools.partial(_rwkv_pre_kernel, tiles_per_seq=seq_len // tm),
        grid=(n_tok // tm,),
        in_specs=[row, prev, _full(mu8.shape)] + [_full(w.shape) for w in ws] + [_full(vecs.shape), _full(bd.shape)],
        out_specs=[row] * 7,
        out_shape=[out] * 7,
        compiler_params=_params("parallel"),
        name="rwkv_pre",
    )(x2d, x2d, mu8, *ws, vecs, bd)


def _rwkv_scan_kernel(r_ref, lw_ref, k_ref, v_ref, na_ref, b_ref, g_ref, vec_ref, o_ref, s_ref):
    R, C = r_ref.shape
    L = SCAN_CHUNK
    N = HEAD_DIM
    PW = 2 * N
    assert PW == V7X_LANES and 2 * L == PW and R % L == 0
    subs = range(R // L)
    rs = [slice(c * L, (c + 1) * L) for c in subs]

    @pl.when(pl.program_id(1) == 0)
    def _():
        s_ref[...] = jnp.zeros_like(s_ref)

    def head_a(width, n_rows=L):
        return (lax.broadcasted_iota(jnp.int32, (n_rows, width), 1) & N) == 0

    def split_ab(z):
        m = head_a(z.shape[1])
        return jnp.concatenate([jnp.where(m, z, 0.0), jnp.where(m, 0.0, z)], axis=0)

    def split_ba(z):
        m = head_a(z.shape[1])
        return jnp.concatenate([jnp.where(m, 0.0, z), jnp.where(m, z, 0.0)], axis=0)

    rows = lax.broadcasted_iota(jnp.int32, (R, R), 0)
    cols = lax.broadcasted_iota(jnp.int32, (R, R), 1)
    lw = lw_ref[...]
    tri = jnp.where(jnp.logical_and(rows >= cols, (rows ^ cols) < L), 1.0, 0.0).astype(BF16)
    h1 = lw.astype(BF16)
    r1 = lw - h1.astype(F32)
    h2 = r1.astype(BF16)
    h3 = (r1 - h2.astype(F32)).astype(BF16)
    cs = _dot(tri, h1) + _dot(tri, h2) + _dot(tri, h3)
    cs_end = [cs[(c + 1) * L - 1:(c + 1) * L, :] for c in subs]
    g_incl = jnp.exp(cs)
    g_excl = jnp.exp(cs - lw)
    g_inv = jnp.exp(-cs)
    g_tail = jnp.exp(jnp.concatenate([jnp.broadcast_to(e, (L, C)) for e in cs_end], axis=0) - cs)
    g_last = [jnp.exp(e) for e in cs_end]

    r = r_ref[...]
    k = k_ref[...]
    v = v_ref[...]
    b = b_ref[...]
    at = na_ref[...] * g_excl
    rt = r * g_incl
    bt = b * g_inv
    kt = k * g_inv
    bh = b * g_tail
    kh = k * g_tail

    n_pairs = C // PW
    units = range(len(subs) * n_pairs)
    sls = [(rs[i // n_pairs], slice((i % n_pairs) * PW, (i % n_pairs + 1) * PW)) for i in units]
    pairs = units
    t_idx = lax.broadcasted_iota(jnp.int32, (L, PW), 0)
    s_idx = lax.broadcasted_iota(jnp.int32, (L, PW), 1) & (N - 1)
    strict2 = t_idx > s_idx
    incl2 = t_idx >= s_idx
    is_a = head_a(PW)
    is_a2 = head_a(PW, 2 * L)

    lhs = [jnp.concatenate([at[sl], rt[sl]], axis=0) for sl in sls]
    g_a = [_dot_nt(_bf(jnp.where(is_a2, lhs[j], 0.0)), _bf(jnp.concatenate([bt[sls[j]], kt[sls[j]]], axis=0)))
           for j in pairs]
    g_b = [_dot_nt(_bf(jnp.where(is_a2, 0.0, lhs[j])), _bf(jnp.concatenate([kt[sls[j]], bt[sls[j]]], axis=0)))
           for j in pairs]
    pp = [jnp.where(strict2, jnp.where(is_a, g_a[j][:L], g_b[j][:L]), 0.0) for j in pairs]
    m_ak = [jnp.where(strict2, jnp.where(is_a, g_b[j][:L], g_a[j][:L]), 0.0) for j in pairs]
    w_rb = [jnp.where(incl2, jnp.where(is_a, g_a[j][L:], g_b[j][L:]), 0.0) for j in pairs]
    w_rk = [jnp.where(incl2, jnp.where(is_a, g_b[j][L:], g_a[j][L:]), 0.0) for j in pairs]
    mv = [_dot(_bf(m_ak[j]), _bf(split_ba(v[sls[j]]))) for j in pairs]
    xs = [jnp.concatenate([at[sls[j]], mv[j]], axis=1) for j in pairs]
    levels = int(math.log2(L))
    for lvl in range(levels):
        if lvl < levels - 1:
            z = [_dot(_bf(pp[j]), _bf(split_ab(jnp.concatenate([xs[j], pp[j]], axis=1)))) for j in pairs]
            xs = [xs[j] + z[j][:, :2 * PW] for j in pairs]
            pp = [z[j][:, 2 * PW:] for j in pairs]
        else:
            xs = [xs[j] + _dot(_bf(pp[j]), _bf(split_ab(xs[j]))) for j in pairs]
    zy = [_dot(_bf(jnp.concatenate([w_rb[j], w_rk[j]], axis=1)),
               _bf(jnp.concatenate([split_ab(xs[j]),
                                    split_ba(jnp.concatenate([jnp.zeros((L, PW), F32), v[sls[j]]], axis=1))], axis=0)))
          for j in pairs]
    vr = lax.broadcasted_iota(jnp.int32, (PW, PW), 0)
    kc = lax.broadcasted_iota(jnp.int32, (PW, PW), 1)
    same_head = (vr & N) == (kc & N)
    state = [s_ref[p] for p in range(n_pairs)]
    y_rows = []
    for c in subs:
        ids = [c * n_pairs + p for p in range(n_pairs)]
        uy = [_dot_nt(_bf(jnp.concatenate([xs[j][:, :PW], rt[sls[j]] + zy[j][:, :PW]], axis=0)), _bf(state[j % n_pairs]))
              for j in ids]
        u = [uy[p][:L] + xs[j][:, PW:] for p, j in enumerate(ids)]
        y_rows.append(jnp.concatenate([uy[p][L:] + zy[j][:, PW:] for p, j in enumerate(ids)], axis=1))
        upd = [_dot_tn(_bf(jnp.concatenate([u[p], v[sls[j]]], axis=0)),
                       _bf(jnp.concatenate([bh[sls[j]], kh[sls[j]]], axis=0))) for p, j in enumerate(ids)]
        state = [state[p] * g_last[c][:, sls[j][1]] + jnp.where(same_head, upd[p], 0.0) for p, j in enumerate(ids)]
    for p in range(n_pairs):
        s_ref[p] = state[p]

    yy = jnp.concatenate(y_rows, axis=0)
    gw = 2 * PW
    jr = lax.broadcasted_iota(jnp.int32, (gw, gw), 0)
    jc = lax.broadcasted_iota(jnp.int32, (gw, gw), 1)
    ones_bd = jnp.where((jr ^ jc) < N, 1.0, 0.0).astype(BF16)

    def head_sum(z):
        hi, lo = _split2(z)
        return jnp.concatenate([_dot(hi[:, c0:c0 + gw], ones_bd) + _dot(lo[:, c0:c0 + gw], ones_bd)
                                for c0 in range(0, C, gw)], axis=1)

    yd = yy - head_sum(yy) * (1.0 / N)
    yv = head_sum(yd * yd) * (1.0 / N)
    yn = yd * lax.rsqrt(yv + RW_GN_EPS) * vec_ref[0:1, :] + vec_ref[1:2, :]
    bonus = head_sum(r * k * vec_ref[2:3, :]) * v
    o_ref[...] = (yn + bonus) * g_ref[...]


def _rwkv_scan(r, lw, k, v, na, b, g, lnx_g, lnx_b, r_k, batch, seq_len):
    n_tok, c = r.shape
    rows = SCAN_CHUNK * SCAN_SUB
    assert seq_len % rows == 0
    vecs = jnp.zeros((V7X_SUBLANES, c), F32).at[0].set(lnx_g).at[1].set(lnx_b).at[2].set(r_k.reshape(c))
    chunks = seq_len // rows
    blk = pl.BlockSpec((rows, c), lambda bi, ci: (bi * chunks + ci, 0))
    return pl.pallas_call(
        _rwkv_scan_kernel,
        grid=(batch, chunks),
        in_specs=[blk] * 7 + [_full(vecs.shape)],
        out_specs=blk,
        out_shape=jax.ShapeDtypeStruct((n_tok, c), F32),
        scratch_shapes=[pltpu.VMEM((c // V7X_LANES, V7X_LANES, V7X_LANES), F32)],
        compiler_params=_params("parallel", "arbitrary"),
        name="rwkv_scan",
    )(r, lw, k, v, na, b, g, vecs)


def _proj_ln_kernel(x_ref, z_ref, w_ref, gb_ref, o_ref):
    m = _dot(_bf(z_ref[...]), w_ref[...])
    o_ref[...] = _layer_norm(DEEPNORM_ALPHA * x_ref[...] + m, gb_ref[0:1, :], gb_ref[1:2, :])


def _proj_ln(x2d, z2d, w, ln_g, ln_b):
    n_tok, c = x2d.shape
    tm = min(2 * ROW_TILE, n_tok)
    gb = jnp.zeros((V7X_SUBLANES, c), F32).at[0].set(ln_g).at[1].set(ln_b)
    row = pl.BlockSpec((tm, c), lambda i: (i, 0))
    return pl.pallas_call(
        _proj_ln_kernel,
        grid=(n_tok // tm,),
        in_specs=[row, row, _full(w.shape), _full(gb.shape)],
        out_specs=row,
        out_shape=jax.ShapeDtypeStruct((n_tok, c), F32),
        compiler_params=_params("parallel"),
        name="proj_ln",
    )(x2d, z2d, _bf(w), gb)


def _router_kernel(x_ref, wt_ref, b_ref, earlier_ref, e_ref, gate_ref, rank_ref, cnt_ref):
    x_hi, x_lo = _split2(x_ref[...])
    w_hi, w_lo = _split2(wt_ref[...])
    logits = _dot_nt(w_hi, x_hi) + _dot_nt(w_hi, x_lo) + _dot_nt(w_lo, x_hi) + b_ref[...]
    n_e = logits.shape[0]
    eid = lax.broadcasted_iota(jnp.int32, logits.shape, 0)
    tops, idxs, hots = [], [], []
    for _ in range(TOP_K):
        m = jnp.max(logits, axis=0, keepdims=True)
        idx = jnp.min(jnp.where(logits == m, eid, n_e), axis=0, keepdims=True)
        tops.append(m)
        idxs.append(idx)
        hots.append(eid == idx)
        logits = jnp.where(eid == idx, -jnp.inf, logits)
    ex = [jnp.exp(t - tops[0]) for t in tops]
    denom = ex[0] + ex[1] + ex[2] + ex[3]
    e_ref[...] = jnp.concatenate(idxs, axis=0)
    gate_ref[...] = jnp.concatenate([e / denom for e in ex], axis=0)
    chosen = sum(jnp.where(hot, 1.0, 0.0) for hot in hots)
    before = _dot(_bf(chosen), earlier_ref[...])
    rank_ref[...] = jnp.concatenate([jnp.sum(jnp.where(hot, before, 0.0), axis=0, keepdims=True) for hot in hots],
                                    axis=0).astype(jnp.int32)
    cnt_ref[...] = jnp.broadcast_to(jnp.sum(chosen, axis=1, keepdims=True), cnt_ref.shape).astype(jnp.int32)


def _router(x2d, w_router, b_router):
    n_tok, c = x2d.shape
    tm = min(MOE_ROUTER_TILE, n_tok)
    wt = w_router.T
    bcol = b_router.reshape(N_EXPERTS, 1)
    tok = jnp.arange(tm, dtype=jnp.int32)
    earlier = (tok[:, None] < tok[None, :]).astype(BF16)
    tile_of = lambda i: (0, i)
    return pl.pallas_call(
        _router_kernel,
        grid=(n_tok // tm,),
        in_specs=[pl.BlockSpec((tm, c), lambda i: (i, 0)), _full(wt.shape), _full(bcol.shape), _full(earlier.shape)],
        out_specs=[pl.BlockSpec((TOP_K, tm), tile_of)] * 3 + [pl.BlockSpec((None, N_EXPERTS, V7X_LANES), lambda i: (i, 0, 0))],
        out_shape=[jax.ShapeDtypeStruct((TOP_K, n_tok), jnp.int32), jax.ShapeDtypeStruct((TOP_K, n_tok), F32),
                   jax.ShapeDtypeStruct((TOP_K, n_tok), jnp.int32),
                   jax.ShapeDtypeStruct((n_tok // tm, N_EXPERTS, V7X_LANES), jnp.int32)],
        compiler_params=_params("parallel"),
        name="moe_router",
    )(x2d, wt, bcol, earlier)


def _expert_kernel(blk_e_ref, n_used_ref, x_ref, wgu_ref, bgu_ref, wd_ref, bd_ref, y_ref, wgu_bf, wd_bf):
    i = pl.program_id(0)
    e = blk_e_ref[i]
    changed = jnp.logical_or(i == 0, e != blk_e_ref[jnp.maximum(i - 1, 0)])

    @pl.when(changed)
    def _():
        wgu_bf[...] = _bf(wgu_ref[...])
        wd_bf[...] = _bf(wd_ref[...])

    @pl.when(i < n_used_ref[0])
    def _():
        f = wd_ref.shape[0]
        gu = _dot(_bf(x_ref[...]), wgu_bf[...]) + bgu_ref[...]
        glu = jnp.minimum(gu[:, :f], SWIGLU_LIMIT)
        lin = jnp.clip(gu[:, f:], -SWIGLU_LIMIT, SWIGLU_LIMIT)
        act = glu * _sigmoid(SWIGLU_ALPHA * glu) * (lin + 1.0)
        y_ref[...] = _dot(_bf(act), wd_bf[...]) + bd_ref[...]

    @pl.when(i >= n_used_ref[0])
    def _():
        y_ref[...] = jnp.zeros_like(y_ref)


def _experts(x_rows, blk_e, n_used, w_gu, b_gu, w_down, b_down, layer):
    n_rows, c = x_rows.shape
    n_l, n_e, _, f2 = w_gu.shape
    f = f2 // 2
    tb = MOE_TILE
    grid_spec = pltpu.PrefetchScalarGridSpec(
        num_scalar_prefetch=2,
        grid=(n_rows // tb,),
        in_specs=[
            pl.BlockSpec((tb, c), lambda i, be, nu: (jnp.minimum(i, nu[0] - 1), 0)),
            pl.BlockSpec((None, None, c, f2), lambda i, be, nu: (layer, be[i], 0, 0)),
            pl.BlockSpec((None, None, 1, f2), lambda i, be, nu: (layer, be[i], 0, 0)),
            pl.BlockSpec((None, None, f, c), lambda i, be, nu: (layer, be[i], 0, 0)),
            pl.BlockSpec((None, None, 1, c), lambda i, be, nu: (layer, be[i], 0, 0)),
        ],
        out_specs=pl.BlockSpec((tb, c), lambda i, be, nu: (i, 0)),
        scratch_shapes=[pltpu.VMEM((c, f2), BF16), pltpu.VMEM((f, c), BF16)],
    )
    return pl.pallas_call(
        _expert_kernel,
        grid_spec=grid_spec,
        out_shape=jax.ShapeDtypeStruct((n_rows, c), F32),
        compiler_params=_params("arbitrary"),
        name="moe_experts",
    )(blk_e, n_used, x_rows, w_gu, b_gu.reshape(n_l, n_e, 1, f2), w_down, b_down.reshape(n_l, n_e, 1, c))


def _combine_ln_kernel(x_ref, y_ref, gate_ref, gb_ref, o_ref):
    acc = DEEPNORM_ALPHA * x_ref[...]
    for j in range(TOP_K):
        acc = acc + gate_ref[:, j:j + 1] * y_ref[j]
    o_ref[...] = _layer_norm(acc, gb_ref[0:1, :], gb_ref[1:2, :])


def _combine_ln(x2d, y4, gates, ln_g, ln_b):
    n_tok, c = x2d.shape
    tm = min(ROW_TILE, n_tok)
    gb = jnp.zeros((V7X_SUBLANES, c), F32).at[0].set(ln_g).at[1].set(ln_b)
    return pl.pallas_call(
        _combine_ln_kernel,
        grid=(n_tok // tm,),
        in_specs=[pl.BlockSpec((tm, c), lambda i: (i, 0)), pl.BlockSpec((TOP_K, tm, c), lambda i: (0, i, 0)),
                  pl.BlockSpec((tm, TOP_K), lambda i: (i, 0)), _full(gb.shape)],
        out_specs=pl.BlockSpec((tm, c), lambda i: (i, 0)),
        out_shape=jax.ShapeDtypeStruct((n_tok, c), F32),
        compiler_params=_params("parallel"),
        name="moe_combine_ln",
    )(x2d, y4, gates, gb)


def _moe(x2d, w_router, b_router, w_gu, b_gu, w_down, b_down, ln_g, ln_b, layer):
    n_tok, c = x2d.shape
    top_e, gates, rank, tile_cnt = _router(x2d, w_router, b_router)
    tb = MOE_TILE
    tile_cnt = tile_cnt[:, :, 0]
    n_tiles = tile_cnt.shape[0]
    counts = jnp.sum(tile_cnt, axis=0)
    padded = (counts + tb - 1) // tb * tb
    pad_end = jnp.cumsum(padded)
    pad_start = pad_end - padded
    base = pad_start[None, :] + jnp.cumsum(tile_cnt, axis=0) - tile_cnt
    base_tok = jnp.broadcast_to(base[:, None, :], (n_tiles, n_tok // n_tiles, N_EXPERTS)).reshape(n_tok, N_EXPERTS)
    hot = top_e[:, :, None] == jnp.arange(N_EXPERTS, dtype=jnp.int32)
    dest = jnp.sum(jnp.where(hot, base_tok[None], 0), axis=-1) + rank
    n_rows = -(-(n_tok * TOP_K + N_EXPERTS * (tb - 1)) // tb) * tb
    blk_start = jnp.arange(n_rows // tb, dtype=jnp.int32) * tb
    blk_e = jnp.minimum(jnp.sum((pad_end[None, :] <= blk_start[:, None]).astype(jnp.int32), axis=1), N_EXPERTS - 1)
    n_used = (pad_end[-1] // tb).astype(jnp.int32).reshape(1)
    tok = jnp.broadcast_to(jnp.arange(n_tok, dtype=jnp.int32)[None, :], (TOP_K, n_tok))
    rows_tok = jnp.zeros((n_rows,), jnp.int32).at[dest.reshape(-1)].set(
        tok.reshape(-1), unique_indices=True, mode="promise_in_bounds")
    x_rows = x2d.at[rows_tok].get(mode="promise_in_bounds")
    y_rows = _experts(x_rows, blk_e, n_used, w_gu, b_gu, w_down, b_down, layer)
    y4 = y_rows.at[dest.reshape(-1)].get(unique_indices=True, mode="promise_in_bounds").reshape(TOP_K, n_tok, c)
    return _combine_ln(x2d, y4, gates.T, ln_g, ln_b)


def _rope(z, cos, sin_signed):
    half = HEAD_DIM // 2
    width = z.shape[-1]
    lane = lax.broadcasted_iota(jnp.int32, z.shape, z.ndim - 1)
    swapped = jnp.where(lane % HEAD_DIM < half, pltpu.roll(z, width - half, z.ndim - 1), pltpu.roll(z, half, z.ndim - 1))
    return z * cos + swapped * sin_signed


def _dsa_proj_kernel(x_ref, w_ref, cos_ref, sin_ref, gb_ref, q_ref, k_ref, v_ref, qi_ref, ki_ref, wi_ref):
    c = x_ref.shape[1]
    proj = _dot(_bf(x_ref[...]), w_ref[...])
    cos2 = cos_ref[...]
    sin2 = sin_ref[...]
    q_w = q_ref.shape[1]
    qi_w = qi_ref.shape[1]
    q = proj[:, :q_w]
    q_ref[...] = _bf(_rope(q, jnp.tile(cos2, (1, q_w // V7X_LANES)), jnp.tile(sin2, (1, q_w // V7X_LANES))))
    kv = proj[:, q_w:q_w + V7X_LANES]
    k_ref[...] = _bf(_rope(kv, cos2, sin2)[:, :HEAD_DIM])
    v_ref[...] = _bf(kv[:, HEAD_DIM:])
    o = q_w + V7X_LANES
    qi = proj[:, o:o + qi_w]
    qi_ref[...] = _rope(qi, jnp.tile(cos2, (1, qi_w // V7X_LANES)), jnp.tile(sin2, (1, qi_w // V7X_LANES)))
    o = o + qi_w
    kiw = proj[:, o:o + V7X_LANES]
    in_ki = lax.broadcasted_iota(jnp.int32, kiw.shape, 1) < IDX_DIM
    mu = jnp.sum(kiw, axis=-1, keepdims=True) * (1.0 / IDX_DIM)
    d = jnp.where(in_ki, kiw - mu, 0.0)
    var = jnp.sum(d * d, axis=-1, keepdims=True) * (1.0 / IDX_DIM)
    ki = d * lax.rsqrt(var + LN_EPS) * gb_ref[0:1, :] + gb_ref[1:2, :]
    ki_ref[...] = _rope(ki, cos2, sin2)[:, :IDX_DIM]
    o = o + V7X_LANES
    wi_ref[...] = proj[:, o:o + IDX_HEADS] * IDX_W_SCALE


def _dsa_proj(x2d, seq_len, w_in, idx_g, idx_b, cos2, sin2):
    n_tok, c = x2d.shape
    q_w = N_HEADS * HEAD_DIM
    qi_w = IDX_HEADS * IDX_DIM
    o_ki = q_w + 2 * HEAD_DIM + qi_w
    w_pad = jnp.concatenate([w_in[:, :o_ki + IDX_DIM], jnp.zeros((c, V7X_LANES - IDX_DIM), F32),
                             w_in[:, o_ki + IDX_DIM:], jnp.zeros((c, V7X_LANES - IDX_HEADS), F32)], axis=1)
    tm = min(ROW_TILE, seq_len)
    tiles = seq_len // tm
    gb = jnp.zeros((V7X_SUBLANES, V7X_LANES), F32).at[0, :IDX_DIM].set(idx_g).at[1, :IDX_DIM].set(idx_b)
    row = lambda w: pl.BlockSpec((tm, w), lambda i: (i, 0))
    tab = pl.BlockSpec((tm, V7X_LANES), lambda i: (i % tiles, 0))
    shapes = [(q_w, BF16), (HEAD_DIM, BF16), (HEAD_DIM, BF16), (qi_w, F32), (IDX_DIM, F32), (IDX_HEADS, F32)]
    return pl.pallas_call(
        _dsa_proj_kernel,
        grid=(n_tok // tm,),
        in_specs=[row(c), _full(w_pad.shape), tab, tab, _full(gb.shape)],
        out_specs=[row(w) for w, _ in shapes],
        out_shape=[jax.ShapeDtypeStruct((n_tok, w), dt) for w, dt in shapes],
        compiler_params=_params("parallel"),
        name="dsa_proj",
    )(x2d, _bf(w_pad), cos2, sin2, gb)


def _dsa_attn_kernel(x_ref, q_ref, qi_ref, wi_ref, ki_ref, k_ref, v_ref, wo_ref, gb_ref, o_ref, *, q_block0, n_keys, k_sel):
    qb = q_block0 + pl.program_id(1)
    tq = q_ref.shape[0]
    nk = n_keys

    ki_hi, ki_lo = _split2(ki_ref[0:nk, :])
    ki3 = jnp.concatenate([ki_hi, ki_hi, ki_lo], axis=1)
    qi = qi_ref[...]
    wi = wi_ref[...]
    s = jnp.zeros((tq, nk), F32)
    for h in range(IDX_HEADS):
        q_hi, q_lo = _split2(qi[:, h * IDX_DIM:(h + 1) * IDX_DIM])
        sh = _dot_nt(jnp.concatenate([q_hi, q_lo, q_hi], axis=1), ki3)
        s = s + jnp.maximum(sh, 0.0) * wi[:, h:h + 1]

    key_pos = lax.broadcasted_iota(jnp.int32, (tq, nk), 1)
    q_pos = qb * tq + lax.broadcasted_iota(jnp.int32, (tq, nk), 0)
    causal = key_pos <= q_pos

    bits = pltpu.bitcast(s, jnp.int32)
    keys = jnp.where(causal, bits ^ ((bits >> 31) & jnp.int32(0x7FFFFFFF)), jnp.int32(INT_MIN))

    step_bits = 2 if nk <= DSA_TWO_BIT_SEARCH_KEYS else 1

    def search_step(i, tau):
        shift = 32 - step_bits * (i + 1)
        settled = jnp.zeros((tq, 1), jnp.int32)
        for c in range(1, 1 << step_bits):
            cand = tau | jnp.left_shift(jnp.int32(c), shift)
            cnt = jnp.sum(jnp.where(keys >= (cand ^ jnp.int32(INT_MIN)), 1.0, 0.0), axis=1, keepdims=True)
            settled = settled + jnp.where(cnt >= k_sel, 1, 0)
        return tau | jnp.left_shift(settled, shift)

    tau = lax.fori_loop(0, 32 // step_bits, search_step, jnp.zeros((tq, 1), jnp.int32),
                        unroll=nk <= DSA_UNROLLED_SEARCH_KEYS) ^ jnp.int32(INT_MIN)
    above = keys > tau
    tied = keys == tau
    room = k_sel - jnp.sum(jnp.where(above, 1.0, 0.0), axis=1, keepdims=True)
    tied_bf = jnp.where(tied, 1.0, 0.0).astype(BF16)
    cr = lax.broadcasted_iota(jnp.int32, (V7X_LANES, V7X_LANES), 0)
    cc = lax.broadcasted_iota(jnp.int32, (V7X_LANES, V7X_LANES), 1)
    earlier_bf = jnp.where(cr < cc, 1.0, 0.0).astype(BF16)
    carry = jnp.zeros((tq, 1), F32)
    before = []
    for ch in range(nk // V7X_LANES):
        t = tied_bf[:, ch * V7X_LANES:(ch + 1) * V7X_LANES]
        before.append(_dot(t, earlier_bf) + carry)
        carry = carry + jnp.sum(t.astype(F32), axis=1, keepdims=True)
    before = jnp.concatenate(before, axis=1)
    sel = jnp.logical_and(jnp.logical_or(above, jnp.logical_and(tied, before < room)), causal)
    bias = jnp.where(sel, 0.0, -1e30)

    k = k_ref[0:nk, :]
    v = v_ref[0:nk, :]
    q = q_ref[...] * (HEAD_DIM ** -0.5)
    n_heads = q.shape[1] // HEAD_DIM
    outs = []
    for g0 in range(0, n_heads, DSA_HEAD_GROUP):
        hs = range(g0, g0 + DSA_HEAD_GROUP)
        logits = _dot_nt(jnp.concatenate([q[:, h * HEAD_DIM:(h + 1) * HEAD_DIM] for h in hs], axis=0), k)
        ps, inv = [], []
        for i in range(DSA_HEAD_GROUP):
            lg = logits[i * tq:(i + 1) * tq, :] + bias
            p = jnp.exp(lg - jnp.max(lg, axis=1, keepdims=True))
            inv.append(1.0 / jnp.sum(p, axis=1, keepdims=True))
            ps.append(_bf(p))
        o = _dot(jnp.concatenate(ps, axis=0), v)
        outs += [o[i * tq:(i + 1) * tq, :] * inv[i] for i in range(DSA_HEAD_GROUP)]
    att = jnp.concatenate(outs, axis=1)
    m_out = _dot(_bf(att), wo_ref[...])
    o_ref[...] = _layer_norm(DEEPNORM_ALPHA * x_ref[...] + m_out, gb_ref[0:1, :], gb_ref[1:2, :])


def _dsa_attn(x2d, batch, seq_len, q, k, v, qi, ki, wi, w_o, ln_g, ln_b):
    n_tok, c = x2d.shape
    tq = Q_TILE
    k_sel = min(TOPK_MAX, seq_len // 4)
    kg = min(KEY_GROUP, seq_len)
    assert seq_len % kg == 0 and kg % tq == 0
    qb_per_group = kg // tq
    qb_per_seq = seq_len // tq
    ki3 = ki.reshape(batch, seq_len, IDX_DIM)
    k3 = k.reshape(batch, seq_len, HEAD_DIM)
    v3 = v.reshape(batch, seq_len, HEAD_DIM)
    gb = jnp.zeros((V7X_SUBLANES, c), F32).at[0].set(ln_g).at[1].set(ln_b)
    wo = _bf(w_o)
    outs = []
    for grp in range(seq_len // kg):
        q0 = grp * qb_per_group
        n_keys = (grp + 1) * kg
        row = lambda w: pl.BlockSpec((tq, w), lambda bi, j: (bi * qb_per_seq + q0 + j, 0))
        per_batch = lambda w: pl.BlockSpec((None, seq_len, w), lambda bi, j: (bi, 0, 0))
        out = pl.pallas_call(
            functools.partial(_dsa_attn_kernel, q_block0=q0, n_keys=n_keys, k_sel=k_sel),
            grid=(batch, qb_per_group),
            in_specs=[row(c), row(q.shape[1]), row(qi.shape[1]), row(wi.shape[1]),
                      per_batch(IDX_DIM), per_batch(HEAD_DIM), per_batch(HEAD_DIM),
                      _full(wo.shape), _full(gb.shape)],
            out_specs=pl.BlockSpec((None, tq, c), lambda bi, j: (bi, j, 0)),
            out_shape=jax.ShapeDtypeStruct((batch, kg, c), F32),
            compiler_params=_params("parallel", "parallel"),
            name=f"dsa_attn_{n_keys}",
        )(x2d, q, qi, wi, ki3, k3, v3, wo, gb)
        outs.append(out)
    return jnp.concatenate(outs, axis=1).reshape(n_tok, c)


def _rope_tables(seq_len):
    pos = jnp.arange(seq_len, dtype=F32)
    inv = 1.0 / (ROPE_THETA ** (jnp.arange(0, HEAD_DIM, 2, dtype=F32) / HEAD_DIM))
    ang = pos[:, None] * inv[None, :]
    cos, sin = jnp.cos(ang), jnp.sin(ang)
    cos2 = jnp.tile(jnp.concatenate([cos, cos], axis=1), (1, V7X_LANES // HEAD_DIM))
    sin2 = jnp.tile(jnp.concatenate([-sin, sin], axis=1), (1, V7X_LANES // HEAD_DIM))
    return cos2, sin2


def kernel(x, rw_mu, rw_w_rkv, rw_w0, rw_w1, rw_w2, rw_a0, rw_a1, rw_a2, rw_g1, rw_g2, rw_k_k, rw_k_a, rw_r_k, rw_lnx_g, rw_lnx_b, rw_w_o, ds_w_in, ds_idx_ln_g, ds_idx_ln_b, ds_w_o, ln_mix_g, ln_mix_b, moe_w_router, moe_b_router, moe_w_gu, moe_b_gu, moe_w_down, moe_b_down, ln_ffn_g, ln_ffn_b):
    batch, seq_len, c = x.shape
    assert c == D_MODEL
    h = x.reshape(batch * seq_len, c)
    cos2, sin2 = _rope_tables(seq_len)
    head_of = jnp.arange(2 * V7X_LANES, dtype=jnp.int32) // HEAD_DIM
    bd = (head_of[:, None] == head_of[None, :]).astype(BF16)
    n_mixers = 2
    for i in range(DEPTH):
        j = i // n_mixers
        if i % n_mixers == 0:
            r, lw, k, v, na, b, g = _rwkv_pre(h, seq_len, rw_mu[j], rw_w_rkv[j], rw_w0[j], rw_w1[j], rw_w2[j],
                                              rw_a0[j], rw_a1[j], rw_a2[j], rw_g1[j], rw_g2[j], rw_k_k[j], rw_k_a[j], bd)
            z = _rwkv_scan(r, lw, k, v, na, b, g, rw_lnx_g[j], rw_lnx_b[j], rw_r_k[j], batch, seq_len)
            h = _proj_ln(h, z, rw_w_o[j], ln_mix_g[i], ln_mix_b[i])
        else:
            q, k, v, qi, ki, wi = _dsa_proj(h, seq_len, ds_w_in[j], ds_idx_ln_g[j], ds_idx_ln_b[j], cos2, sin2)
            h = _dsa_attn(h, batch, seq_len, q, k, v, qi, ki, wi, ds_w_o[j], ln_mix_g[i], ln_mix_b[i])
        h = _moe(h, moe_w_router[i], moe_b_router[i], moe_w_gu, moe_b_gu, moe_w_down, moe_b_down,
                 ln_ffn_g[i], ln_ffn_b[i], i)
    return h.reshape(batch, seq_len, c)
```

```python
import functools
import math

import jax
import jax.numpy as jnp
from jax import lax
from jax.experimental import pallas as pl
from jax.experimental.pallas import tpu as pltpu

D_MODEL = 1024
HEAD_DIM = 64
N_HEADS = D_MODEL // HEAD_DIM
RW_GN_EPS = 64e-5
IDX_HEADS = 8
IDX_DIM = 64
TOPK_MAX = 256
ROPE_THETA = 10000.0
N_EXPERTS = 32
TOP_K = 4
SWIGLU_LIMIT = 7.0
SWIGLU_ALPHA = 1.702
LN_EPS = 1e-5
DEPTH = 2
DEEPNORM_ALPHA = (2.0 * DEPTH) ** 0.25
IDX_W_SCALE = IDX_HEADS ** -0.5 * IDX_DIM ** -0.5

V7X_LANES = 128
V7X_SUBLANES = 8
V7X_VMEM_LIMIT_BYTES = 56 * 1024 * 1024

SCAN_CHUNK = 64
SCAN_SUB = 2
ROW_TILE = 256
Q_TILE = 128
KEY_GROUP = 256
MOE_TILE = 512
MOE_ROUTER_TILE = 1024
DSA_HEAD_GROUP = 4
DSA_TWO_BIT_SEARCH_KEYS = 768
DSA_UNROLLED_SEARCH_KEYS = 1280

F32 = jnp.float32
BF16 = jnp.bfloat16
INT_MIN = -2147483648


def _bf(x):
    return x.astype(BF16)


def _split2(x):
    hi = x.astype(BF16)
    lo = (x - hi.astype(F32)).astype(BF16)
    return hi, lo


def _dot(a, b):
    return jnp.dot(a, b, preferred_element_type=F32)


def _dot_nt(a, b):
    return lax.dot_general(a, b, (((1,), (1,)), ((), ())), preferred_element_type=F32)


def _dot_tn(a, b):
    return lax.dot_general(a, b, (((0,), (0,)), ((), ())), preferred_element_type=F32)


def _layer_norm(h, g, b):
    mu = jnp.mean(h, axis=-1, keepdims=True)
    d = h - mu
    var = jnp.mean(d * d, axis=-1, keepdims=True)
    return d * lax.rsqrt(var + LN_EPS) * g + b


def _sigmoid(z):
    return 1.0 / (1.0 + jnp.exp(-z))


def _params(*sem):
    return pltpu.CompilerParams(dimension_semantics=sem, vmem_limit_bytes=V7X_VMEM_LIMIT_BYTES)


def _full(shape):
    n = len(shape)
    return pl.BlockSpec(shape, lambda *_: (0,) * n)


def _rwkv_pre_kernel(x_ref, xp_ref, mu_ref, wrkv_ref, w1_ref, w2_ref, a1_ref, a2_ref, g1_ref, g2_ref,
                     vec_ref, bd_ref, r_ref, lw_ref, k_ref, v_ref, na_ref, b_ref, g_ref, *, tiles_per_seq):
    i = pl.program_id(0)
    x = x_ref[...]
    prev = xp_ref[V7X_SUBLANES - 1:V7X_SUBLANES, :]
    prev = jnp.where(i % tiles_per_seq == 0, 0.0, prev)
    rows = lax.broadcasted_iota(jnp.int32, x.shape, 0)
    xprev = jnp.where(rows == 0, prev, pltpu.roll(x, 1, 0))
    xx = xprev - x

    def mix(p):
        return x + xx * mu_ref[p:p + 1, :]

    w0 = vec_ref[0:1, :]
    a0 = vec_ref[1:2, :]
    k_k = vec_ref[2:3, :]
    k_a = vec_ref[3:4, :]

    r = _dot(_bf(mix(0)), wrkv_ref[0])
    k = _dot(_bf(mix(1)), wrkv_ref[1])
    v = _dot(_bf(mix(2)), wrkv_ref[2])

    zw = w0 + _dot(_bf(jnp.tanh(_dot(_bf(mix(3)), w1_ref[...]))), w2_ref[...])
    nz = -zw
    softplus = jnp.maximum(nz, 0.0) + jnp.log1p(jnp.exp(-jnp.abs(nz)))
    w_log = -softplus - 0.5
    lw = -jnp.exp(w_log)

    a = _sigmoid(a0 + _dot(_bf(_dot(_bf(mix(4)), a1_ref[...])), a2_ref[...]))
    g = _dot(_bf(_sigmoid(_dot(_bf(mix(5)), g1_ref[...]))), g2_ref[...])

    kk = k * k_k
    sq_hi, sq_lo = _split2(kk * kk)
    gw = bd_ref.shape[0]
    ss = jnp.concatenate([_dot(sq_hi[:, c0:c0 + gw], bd_ref[...]) + _dot(sq_lo[:, c0:c0 + gw], bd_ref[...])
                          for c0 in range(0, kk.shape[1], gw)], axis=1)
    kk = kk / jnp.maximum(jnp.sqrt(ss), 1e-12)
    k = k * (1.0 + (a - 1.0) * k_a)

    r_ref[...] = r
    lw_ref[...] = lw
    k_ref[...] = k
    v_ref[...] = v
    na_ref[...] = -kk
    b_ref[...] = kk * a
    g_ref[...] = g


def _rwkv_pre(x2d, seq_len, mu, w_rkv, w0, w1, w2, a0, a1, a2, g1, g2, k_k, k_a, bd):
    n_tok, c = x2d.shape
    tm = min(ROW_TILE, seq_len)
    assert seq_len % tm == 0 and tm % V7X_SUBLANES == 0
    vecs = jnp.zeros((V7X_SUBLANES, c), F32).at[0].set(w0).at[1].set(a0).at[2].set(k_k).at[3].set(k_a)
    mu8 = jnp.zeros((V7X_SUBLANES, c), F32).at[:6].set(mu)
    row = pl.BlockSpec((tm, c), lambda i: (i, 0))
    sub_per_tile = tm // V7X_SUBLANES
    prev = pl.BlockSpec((V7X_SUBLANES, c), lambda i: (jnp.maximum(i * sub_per_tile - 1, 0), 0))
    ws = [_bf(w_rkv), _bf(w1), _bf(w2), _bf(a1), _bf(a2), _bf(g1), _bf(g2)]
    out = jax.ShapeDtypeStruct((n_tok, c), F32)
    return pl.pallas_call(
        functools.partial(_rwkv_pre_kernel, tiles_per_seq=seq_len // tm),
        grid=(n_tok // tm,),
        in_specs=[row, prev, _full(mu8.shape)] + [_full(w.shape) for w in ws] + [_full(vecs.shape), _full(bd.shape)],
        out_specs=[row] * 7,
        out_shape=[out] * 7,
        compiler_params=_params("parallel"),
        name="rwkv_pre",
    )(x2d, x2d, mu8, *ws, vecs, bd)


def _rwkv_scan_kernel(r_ref, lw_ref, k_ref, v_ref, na_ref, b_ref, g_ref, vec_ref, o_ref, s_ref):
    R, C = r_ref.shape
    L = SCAN_CHUNK
    N = HEAD_DIM
    PW = 2 * N
    assert PW == V7X_LANES and 2 * L == PW and R % L == 0
    subs = range(R // L)
    rs = [slice(c * L, (c + 1) * L) for c in subs]

    @pl.when(pl.program_id(1) == 0)
    def _():
        s_ref[...] = jnp.zeros_like(s_ref)

    def head_a(width, n_rows=L):
        return (lax.broadcasted_iota(jnp.int32, (n_rows, width), 1) & N) == 0

    def split_ab(z):
        m = head_a(z.shape[1])
        return jnp.concatenate([jnp.where(m, z, 0.0), jnp.where(m, 0.0, z)], axis=0)

    def split_ba(z):
        m = head_a(z.shape[1])
        return jnp.concatenate([jnp.where(m, 0.0, z), jnp.where(m, z, 0.0)], axis=0)

    rows = lax.broadcasted_iota(jnp.int32, (R, R), 0)
    cols = lax.broadcasted_iota(jnp.int32, (R, R), 1)
    lw = lw_ref[...]
    tri = jnp.where(jnp.logical_and(rows >= cols, (rows ^ cols) < L), 1.0, 0.0).astype(BF16)
    h1 = lw.astype(BF16)
    r1 = lw - h1.astype(F32)
    h2 = r1.astype(BF16)
    h3 = (r1 - h2.astype(F32)).astype(BF16)
    cs = _dot(tri, h1) + _dot(tri, h2) + _dot(tri, h3)
    cs_end = [cs[(c + 1) * L - 1:(c + 1) * L, :] for c in subs]
    g_incl = jnp.exp(cs)
    g_excl = jnp.exp(cs - lw)
    g_inv = jnp.exp(-cs)
    g_tail = jnp.exp(jnp.concatenate([jnp.broadcast_to(e, (L, C)) for e in cs_end], axis=0) - cs)
    g_last = [jnp.exp(e) for e in cs_end]

    r = r_ref[...]
    k = k_ref[...]
    v = v_ref[...]
    b = b_ref[...]
    at = na_ref[...] * g_excl
    rt = r * g_incl
    bt = b * g_inv
    kt = k * g_inv
    bh = b * g_tail
    kh = k * g_tail

    n_pairs = C // PW
    units = range(len(subs) * n_pairs)
    sls = [(rs[i // n_pairs], slice((i % n_pairs) * PW, (i % n_pairs + 1) * PW)) for i in units]
    pairs = units
    t_idx = lax.broadcasted_iota(jnp.int32, (L, PW), 0)
    s_idx = lax.broadcasted_iota(jnp.int32, (L, PW), 1) & (N - 1)
    strict2 = t_idx > s_idx
    incl2 = t_idx >= s_idx
    is_a = head_a(PW)
    is_a2 = head_a(PW, 2 * L)

    lhs = [jnp.concatenate([at[sl], rt[sl]], axis=0) for sl in sls]
    g_a = [_dot_nt(_bf(jnp.where(is_a2, lhs[j], 0.0)), _bf(jnp.concatenate([bt[sls[j]], kt[sls[j]]], axis=0)))
           for j in pairs]
    g_b = [_dot_nt(_bf(jnp.where(is_a2, 0.0, lhs[j])), _bf(jnp.concatenate([kt[sls[j]], bt[sls[j]]], axis=0)))
           for j in pairs]
    pp = [jnp.where(strict2, jnp.where(is_a, g_a[j][:L], g_b[j][:L]), 0.0) for j in pairs]
    m_ak = [jnp.where(strict2, jnp.where(is_a, g_b[j][:L], g_a[j][:L]), 0.0) for j in pairs]
    w_rb = [jnp.where(incl2, jnp.where(is_a, g_a[j][L:], g_b[j][L:]), 0.0) for j in pairs]
    w_rk = [jnp.where(incl2, jnp.where(is_a, g_b[j][L:], g_a[j][L:]), 0.0) for j in pairs]
    mv = [_dot(_bf(m_ak[j]), _bf(split_ba(v[sls[j]]))) for j in pairs]
    xs = [jnp.concatenate([at[sls[j]], mv[j]], axis=1) for j in pairs]
    levels = int(math.log2(L))
    for lvl in range(levels):
        if lvl < levels - 1:
            z = [_dot(_bf(pp[j]), _bf(split_ab(jnp.concatenate([xs[j], pp[j]], axis=1)))) for j in pairs]
            xs = [xs[j] + z[j][:, :2 * PW] for j in pairs]
            pp = [z[j][:, 2 * PW:] for j in pairs]
        else:
            xs = [xs[j] + _dot(_bf(pp[j]), _bf(split_ab(xs[j]))) for j in pairs]
    zy = [_dot(_bf(jnp.concatenate([w_rb[j], w_rk[j]], axis=1)),
               _bf(jnp.concatenate([split_ab(xs[j]),
                                    split_ba(jnp.concatenate([jnp.zeros((L, PW), F32), v[sls[j]]], axis=1))], axis=0)))
          for j in pairs]
    vr = lax.broadcasted_iota(jnp.int32, (PW, PW), 0)
    kc = lax.broadcasted_iota(jnp.int32, (PW, PW), 1)
    same_head = (vr & N) == (kc & N)
    state = [s_ref[p] for p in range(n_pairs)]
    y_rows = []
    for c in subs:
        ids = [c * n_pairs + p for p in range(n_pairs)]
        uy = [_dot_nt(_bf(jnp.concatenate([xs[j][:, :PW], rt[sls[j]] + zy[j][:, :PW]], axis=0)), _bf(state[j % n_pairs]))
              for j in ids]
        u = [uy[p][:L] + xs[j][:, PW:] for p, j in enumerate(ids)]
        y_rows.append(jnp.concatenate([uy[p][L:] + zy[j][:, PW:] for p, j in enumerate(ids)], axis=1))
        upd = [_dot_tn(_bf(jnp.concatenate([u[p], v[sls[j]]], axis=0)),
                       _bf(jnp.concatenate([bh[sls[j]], kh[sls[j]]], axis=0))) for p, j in enumerate(ids)]
        state = [state[p] * g_last[c][:, sls[j][1]] + jnp.where(same_head, upd[p], 0.0) for p, j in enumerate(ids)]
    for p in range(n_pairs):
        s_ref[p] = state[p]

    yy = jnp.concatenate(y_rows, axis=0)
    gw = 2 * PW
    jr = lax.broadcasted_iota(jnp.int32, (gw, gw), 0)
    jc = lax.broadcasted_iota(jnp.int32, (gw, gw), 1)
    ones_bd = jnp.where((jr ^ jc) < N, 1.0, 0.0).astype(BF16)

    def head_sum(z):
        hi, lo = _split2(z)
        return jnp.concatenate([_dot(hi[:, c0:c0 + gw], ones_bd) + _dot(lo[:, c0:c0 + gw], ones_bd)
                                for c0 in range(0, C, gw)], axis=1)

    yd = yy - head_sum(yy) * (1.0 / N)
    yv = head_sum(yd * yd) * (1.0 / N)
    yn = yd * lax.rsqrt(yv + RW_GN_EPS) * vec_ref[0:1, :] + vec_ref[1:2, :]
    bonus = head_sum(r * k * vec_ref[2:3, :]) * v
    o_ref[...] = (yn + bonus) * g_ref[...]


def _rwkv_scan(r, lw, k, v, na, b, g, lnx_g, lnx_b, r_k, batch, seq_len):
    n_tok, c = r.shape
    rows = SCAN_CHUNK * SCAN_SUB
    assert seq_len % rows == 0
    vecs = jnp.zeros((V7X_SUBLANES, c), F32).at[0].set(lnx_g).at[1].set(lnx_b).at[2].set(r_k.reshape(c))
    chunks = seq_len // rows
    blk = pl.BlockSpec((rows, c), lambda bi, ci: (bi * chunks + ci, 0))
    return pl.pallas_call(
        _rwkv_scan_kernel,
        grid=(batch, chunks),
        in_specs=[blk] * 7 + [_full(vecs.shape)],
        out_specs=blk,
        out_shape=jax.ShapeDtypeStruct((n_tok, c), F32),
        scratch_shapes=[pltpu.VMEM((c // V7X_LANES, V7X_LANES, V7X_LANES), F32)],
        compiler_params=_params("parallel", "arbitrary"),
        name="rwkv_scan",
    )(r, lw, k, v, na, b, g, vecs)


def _proj_ln_kernel(x_ref, z_ref, w_ref, gb_ref, o_ref):
    m = _dot(_bf(z_ref[...]), w_ref[...])
    o_ref[...] = _layer_norm(DEEPNORM_ALPHA * x_ref[...] + m, gb_ref[0:1, :], gb_ref[1:2, :])


def _proj_ln(x2d, z2d, w, ln_g, ln_b):
    n_tok, c = x2d.shape
    tm = min(2 * ROW_TILE, n_tok)
    gb = jnp.zeros((V7X_SUBLANES, c), F32).at[0].set(ln_g).at[1].set(ln_b)
    row = pl.BlockSpec((tm, c), lambda i: (i, 0))
    return pl.pallas_call(
        _proj_ln_kernel,
        grid=(n_tok // tm,),
        in_specs=[row, row, _full(w.shape), _full(gb.shape)],
        out_specs=row,
        out_shape=jax.ShapeDtypeStruct((n_tok, c), F32),
        compiler_params=_params("parallel"),
        name="proj_ln",
    )(x2d, z2d, _bf(w), gb)


def _router_kernel(x_ref, wt_ref, b_ref, earlier_ref, e_ref, gate_ref, rank_ref, cnt_ref):
    x_hi, x_lo = _split2(x_ref[...])
    w_hi, w_lo = _split2(wt_ref[...])
    logits = _dot_nt(w_hi, x_hi) + _dot_nt(w_hi, x_lo) + _dot_nt(w_lo, x_hi) + b_ref[...]
    n_e = logits.shape[0]
    eid = lax.broadcasted_iota(jnp.int32, logits.shape, 0)
    tops, idxs, hots = [], [], []
    for _ in range(TOP_K):
        m = jnp.max(logits, axis=0, keepdims=True)
        idx = jnp.min(jnp.where(logits == m, eid, n_e), axis=0, keepdims=True)
        tops.append(m)
        idxs.append(idx)
        hots.append(eid == idx)
        logits = jnp.where(eid == idx, -jnp.inf, logits)
    ex = [jnp.exp(t - tops[0]) for t in tops]
    denom = ex[0] + ex[1] + ex[2] + ex[3]
    e_ref[...] = jnp.concatenate(idxs, axis=0)
    gate_ref[...] = jnp.concatenate([e / denom for e in ex], axis=0)
    chosen = sum(jnp.where(hot, 1.0, 0.0) for hot in hots)
    before = _dot(_bf(chosen), earlier_ref[...])
    rank_ref[...] = jnp.concatenate([jnp.sum(jnp.where(hot, before, 0.0), axis=0, keepdims=True) for hot in hots],
                                    axis=0).astype(jnp.int32)
    cnt_ref[...] = jnp.broadcast_to(jnp.sum(chosen, axis=1, keepdims=True), cnt_ref.shape).astype(jnp.int32)


def _router(x2d, w_router, b_router):
    n_tok, c = x2d.shape
    tm = min(MOE_ROUTER_TILE, n_tok)
    wt = w_router.T
    bcol = b_router.reshape(N_EXPERTS, 1)
    tok = jnp.arange(tm, dtype=jnp.int32)
    earlier = (tok[:, None] < tok[None, :]).astype(BF16)
    tile_of = lambda i: (0, i)
    return pl.pallas_call(
        _router_kernel,
        grid=(n_tok // tm,),
        in_specs=[pl.BlockSpec((tm, c), lambda i: (i, 0)), _full(wt.shape), _full(bcol.shape), _full(earlier.shape)],
        out_specs=[pl.BlockSpec((TOP_K, tm), tile_of)] * 3 + [pl.BlockSpec((None, N_EXPERTS, V7X_LANES), lambda i: (i, 0, 0))],
        out_shape=[jax.ShapeDtypeStruct((TOP_K, n_tok), jnp.int32), jax.ShapeDtypeStruct((TOP_K, n_tok), F32),
                   jax.ShapeDtypeStruct((TOP_K, n_tok), jnp.int32),
                   jax.ShapeDtypeStruct((n_tok // tm, N_EXPERTS, V7X_LANES), jnp.int32)],
        compiler_params=_params("parallel"),
        name="moe_router",
    )(x2d, wt, bcol, earlier)


def _expert_kernel(blk_e_ref, n_used_ref, x_ref, wgu_ref, bgu_ref, wd_ref, bd_ref, y_ref, wgu_bf, wd_bf):
    i = pl.program_id(0)
    e = blk_e_ref[i]
    changed = jnp.logical_or(i == 0, e != blk_e_ref[jnp.maximum(i - 1, 0)])

    @pl.when(changed)
    def _():
        wgu_bf[...] = _bf(wgu_ref[...])
        wd_bf[...] = _bf(wd_ref[...])

    @pl.when(i < n_used_ref[0])
    def _():
        f = wd_ref.shape[0]
        gu = _dot(_bf(x_ref[...]), wgu_bf[...]) + bgu_ref[...]
        glu = jnp.minimum(gu[:, :f], SWIGLU_LIMIT)
        lin = jnp.clip(gu[:, f:], -SWIGLU_LIMIT, SWIGLU_LIMIT)
        act = glu * _sigmoid(SWIGLU_ALPHA * glu) * (lin + 1.0)
        y_ref[...] = _dot(_bf(act), wd_bf[...]) + bd_ref[...]

    @pl.when(i >= n_used_ref[0])
    def _():
        y_ref[...] = jnp.zeros_like(y_ref)


def _experts(x_rows, blk_e, n_used, w_gu, b_gu, w_down, b_down, layer):
    n_rows, c = x_rows.shape
    n_l, n_e, _, f2 = w_gu.shape
    f = f2 // 2
    tb = MOE_TILE
    grid_spec = pltpu.PrefetchScalarGridSpec(
        num_scalar_prefetch=2,
        grid=(n_rows // tb,),
        in_specs=[
            pl.BlockSpec((tb, c), lambda i, be, nu: (jnp.minimum(i, nu[0] - 1), 0)),
            pl.BlockSpec((None, None, c, f2), lambda i, be, nu: (layer, be[i], 0, 0)),
            pl.BlockSpec((None, None, 1, f2), lambda i, be, nu: (layer, be[i], 0, 0)),
            pl.BlockSpec((None, None, f, c), lambda i, be, nu: (layer, be[i], 0, 0)),
            pl.BlockSpec((None, None, 1, c), lambda i, be, nu: (layer, be[i], 0, 0)),
        ],
        out_specs=pl.BlockSpec((tb, c), lambda i, be, nu: (i, 0)),
        scratch_shapes=[pltpu.VMEM((c, f2), BF16), pltpu.VMEM((f, c), BF16)],
    )
    return pl.pallas_call(
        _expert_kernel,
        grid_spec=grid_spec,
        out_shape=jax.ShapeDtypeStruct((n_rows, c), F32),
        compiler_params=_params("arbitrary"),
        name="moe_experts",
    )(blk_e, n_used, x_rows, w_gu, b_gu.reshape(n_l, n_e, 1, f2), w_down, b_down.reshape(n_l, n_e, 1, c))


def _combine_ln_kernel(x_ref, y_ref, gate_ref, gb_ref, o_ref):
    acc = DEEPNORM_ALPHA * x_ref[...]
    for j in range(TOP_K):
        acc = acc + gate_ref[:, j:j + 1] * y_ref[j]
    o_ref[...] = _layer_norm(acc, gb_ref[0:1, :], gb_ref[1:2, :])


def _combine_ln(x2d, y4, gates, ln_g, ln_b):
    n_tok, c = x2d.shape
    tm = min(ROW_TILE, n_tok)
    gb = jnp.zeros((V7X_SUBLANES, c), F32).at[0].set(ln_g).at[1].set(ln_b)
    return pl.pallas_call(
        _combine_ln_kernel,
        grid=(n_tok // tm,),
        in_specs=[pl.BlockSpec((tm, c), lambda i: (i, 0)), pl.BlockSpec((TOP_K, tm, c), lambda i: (0, i, 0)),
                  pl.BlockSpec((tm, TOP_K), lambda i: (i, 0)), _full(gb.shape)],
        out_specs=pl.BlockSpec((tm, c), lambda i: (i, 0)),
        out_shape=jax.ShapeDtypeStruct((n_tok, c), F32),
        compiler_params=_params("parallel"),
        name="moe_combine_ln",
    )(x2d, y4, gates, gb)


def _moe(x2d, w_router, b_router, w_gu, b_gu, w_down, b_down, ln_g, ln_b, layer):
    n_tok, c = x2d.shape
    top_e, gates, rank, tile_cnt = _router(x2d, w_router, b_router)
    tb = MOE_TILE
    tile_cnt = tile_cnt[:, :, 0]
    n_tiles = tile_cnt.shape[0]
    counts = jnp.sum(tile_cnt, axis=0)
    padded = (counts + tb - 1) // tb * tb
    pad_end = jnp.cumsum(padded)
    pad_start = pad_end - padded
    base = pad_start[None, :] + jnp.cumsum(tile_cnt, axis=0) - tile_cnt
    base_tok = jnp.broadcast_to(base[:, None, :], (n_tiles, n_tok // n_tiles, N_EXPERTS)).reshape(n_tok, N_EXPERTS)
    hot = top_e[:, :, None] == jnp.arange(N_EXPERTS, dtype=jnp.int32)
    dest = jnp.sum(jnp.where(hot, base_tok[None], 0), axis=-1) + rank
    n_rows = -(-(n_tok * TOP_K + N_EXPERTS * (tb - 1)) // tb) * tb
    blk_start = jnp.arange(n_rows // tb, dtype=jnp.int32) * tb
    blk_e = jnp.minimum(jnp.sum((pad_end[None, :] <= blk_start[:, None]).astype(jnp.int32), axis=1), N_EXPERTS - 1)
    n_used = (pad_end[-1] // tb).astype(jnp.int32).reshape(1)
    tok = jnp.broadcast_to(jnp.arange(n_tok, dtype=jnp.int32)[None, :], (TOP_K, n_tok))
    rows_tok = jnp.zeros((n_rows,), jnp.int32).at[dest.reshape(-1)].set(
        tok.reshape(-1), unique_indices=True, mode="promise_in_bounds")
    x_rows = x2d.at[rows_tok].get(mode="promise_in_bounds")
    y_rows = _experts(x_rows, blk_e, n_used, w_gu, b_gu, w_down, b_down, layer)
    y4 = y_rows.at[dest.reshape(-1)].get(unique_indices=True, mode="promise_in_bounds").reshape(TOP_K, n_tok, c)
    return _combine_ln(x2d, y4, gates.T, ln_g, ln_b)


def _rope(z, cos, sin_signed):
    half = HEAD_DIM // 2
    width = z.shape[-1]
    lane = lax.broadcasted_iota(jnp.int32, z.shape, z.ndim - 1)
    swapped = jnp.where(lane % HEAD_DIM < half, pltpu.roll(z, width - half, z.ndim - 1), pltpu.roll(z, half, z.ndim - 1))
    return z * cos + swapped * sin_signed


def _dsa_proj_kernel(x_ref, w_ref, cos_ref, sin_ref, gb_ref, q_ref, k_ref, v_ref, qi_ref, ki_ref, wi_ref):
    c = x_ref.shape[1]
    proj = _dot(_bf(x_ref[...]), w_ref[...])
    cos2 = cos_ref[...]
    sin2 = sin_ref[...]
    q_w = q_ref.shape[1]
    qi_w = qi_ref.shape[1]
    q = proj[:, :q_w]
    q_ref[...] = _bf(_rope(q, jnp.tile(cos2, (1, q_w // V7X_LANES)), jnp.tile(sin2, (1, q_w // V7X_LANES))))
    kv = proj[:, q_w:q_w + V7X_LANES]
    k_ref[...] = _bf(_rope(kv, cos2, sin2)[:, :HEAD_DIM])
    v_ref[...] = _bf(kv[:, HEAD_DIM:])
    o = q_w + V7X_LANES
    qi = proj[:, o:o + qi_w]
    qi_ref[...] = _rope(qi, jnp.tile(cos2, (1, qi_w // V7X_LANES)), jnp.tile(sin2, (1, qi_w // V7X_LANES)))
    o = o + qi_w
    kiw = proj[:, o:o + V7X_LANES]
    in_ki = lax.broadcasted_iota(jnp.int32, kiw.shape, 1) < IDX_DIM
    mu = jnp.sum(kiw, axis=-1, keepdims=True) * (1.0 / IDX_DIM)
    d = jnp.where(in_ki, kiw - mu, 0.0)
    var = jnp.sum(d * d, axis=-1, keepdims=True) * (1.0 / IDX_DIM)
    ki = d * lax.rsqrt(var + LN_EPS) * gb_ref[0:1, :] + gb_ref[1:2, :]
    ki_ref[...] = _rope(ki, cos2, sin2)[:, :IDX_DIM]
    o = o + V7X_LANES
    wi_ref[...] = proj[:, o:o + IDX_HEADS] * IDX_W_SCALE


def _dsa_proj(x2d, seq_len, w_in, idx_g, idx_b, cos2, sin2):
    n_tok, c = x2d.shape
    q_w = N_HEADS * HEAD_DIM
    qi_w = IDX_HEADS * IDX_DIM
    o_ki = q_w + 2 * HEAD_DIM + qi_w
    w_pad = jnp.concatenate([w_in[:, :o_ki + IDX_DIM], jnp.zeros((c, V7X_LANES - IDX_DIM), F32),
                             w_in[:, o_ki + IDX_DIM:], jnp.zeros((c, V7X_LANES - IDX_HEADS), F32)], axis=1)
    tm = min(ROW_TILE, seq_len)
    tiles = seq_len // tm
    gb = jnp.zeros((V7X_SUBLANES, V7X_LANES), F32).at[0, :IDX_DIM].set(idx_g).at[1, :IDX_DIM].set(idx_b)
    row = lambda w: pl.BlockSpec((tm, w), lambda i: (i, 0))
    tab = pl.BlockSpec((tm, V7X_LANES), lambda i: (i % tiles, 0))
    shapes = [(q_w, BF16), (HEAD_DIM, BF16), (HEAD_DIM, BF16), (qi_w, F32), (IDX_DIM, F32), (IDX_HEADS, F32)]
    return pl.pallas_call(
        _dsa_proj_kernel,
        grid=(n_tok // tm,),
        in_specs=[row(c), _full(w_pad.shape), tab, tab, _full(gb.shape)],
        out_specs=[row(w) for w, _ in shapes],
        out_shape=[jax.ShapeDtypeStruct((n_tok, w), dt) for w, dt in shapes],
        compiler_params=_params("parallel"),
        name="dsa_proj",
    )(x2d, _bf(w_pad), cos2, sin2, gb)


def _dsa_attn_kernel(x_ref, q_ref, qi_ref, wi_ref, ki_ref, k_ref, v_ref, wo_ref, gb_ref, o_ref, *, q_block0, n_keys, k_sel):
    qb = q_block0 + pl.program_id(1)
    tq = q_ref.shape[0]
    nk = n_keys

    ki_hi, ki_lo = _split2(ki_ref[0:nk, :])
    ki3 = jnp.concatenate([ki_hi, ki_hi, ki_lo], axis=1)
    qi = qi_ref[...]
    wi = wi_ref[...]
    s = jnp.zeros((tq, nk), F32)
    for h in range(IDX_HEADS):
        q_hi, q_lo = _split2(qi[:, h * IDX_DIM:(h + 1) * IDX_DIM])
        sh = _dot_nt(jnp.concatenate([q_hi, q_lo, q_hi], axis=1), ki3)
        s = s + jnp.maximum(sh, 0.0) * wi[:, h:h + 1]

    key_pos = lax.broadcasted_iota(jnp.int32, (tq, nk), 1)
    q_pos = qb * tq + lax.broadcasted_iota(jnp.int32, (tq, nk), 0)
    causal = key_pos <= q_pos

    bits = pltpu.bitcast(s, jnp.int32)
    keys = jnp.where(causal, bits ^ ((bits >> 31) & jnp.int32(0x7FFFFFFF)), jnp.int32(INT_MIN))

    step_bits = 2 if nk <= DSA_TWO_BIT_SEARCH_KEYS else 1

    def search_step(i, tau):
        shift = 32 - step_bits * (i + 1)
        settled = jnp.zeros((tq, 1), jnp.int32)
        for c in range(1, 1 << step_bits):
            cand = tau | jnp.left_shift(jnp.int32(c), shift)
            cnt = jnp.sum(jnp.where(keys >= (cand ^ jnp.int32(INT_MIN)), 1.0, 0.0), axis=1, keepdims=True)
            settled = settled + jnp.where(cnt >= k_sel, 1, 0)
        return tau | jnp.left_shift(settled, shift)

    tau = lax.fori_loop(0, 32 // step_bits, search_step, jnp.zeros((tq, 1), jnp.int32),
                        unroll=nk <= DSA_UNROLLED_SEARCH_KEYS) ^ jnp.int32(INT_MIN)
    above = keys > tau
    tied = keys == tau
    room = k_sel - jnp.sum(jnp.where(above, 1.0, 0.0), axis=1, keepdims=True)
    tied_bf = jnp.where(tied, 1.0, 0.0).astype(BF16)
    cr = lax.broadcasted_iota(jnp.int32, (V7X_LANES, V7X_LANES), 0)
    cc = lax.broadcasted_iota(jnp.int32, (V7X_LANES, V7X_LANES), 1)
    earlier_bf = jnp.where(cr < cc, 1.0, 0.0).astype(BF16)
    carry = jnp.zeros((tq, 1), F32)
    before = []
    for ch in range(nk // V7X_LANES):
        t = tied_bf[:, ch * V7X_LANES:(ch + 1) * V7X_LANES]
        before.append(_dot(t, earlier_bf) + carry)
        carry = carry + jnp.sum(t.astype(F32), axis=1, keepdims=True)
    before = jnp.concatenate(before, axis=1)
    sel = jnp.logical_and(jnp.logical_or(above, jnp.logical_and(tied, before < room)), causal)
    bias = jnp.where(sel, 0.0, -1e30)

    k = k_ref[0:nk, :]
    v = v_ref[0:nk, :]
    q = q_ref[...] * (HEAD_DIM ** -0.5)
    n_heads = q.shape[1] // HEAD_DIM
    outs = []
    for g0 in range(0, n_heads, DSA_HEAD_GROUP):
        hs = range(g0, g0 + DSA_HEAD_GROUP)
        logits = _dot_nt(jnp.concatenate([q[:, h * HEAD_DIM:(h + 1) * HEAD_DIM] for h in hs], axis=0), k)
        ps, inv = [], []
        for i in range(DSA_HEAD_GROUP):
            lg = logits[i * tq:(i + 1) * tq, :] + bias
            p = jnp.exp(lg - jnp.max(lg, axis=1, keepdims=True))
            inv.append(1.0 / jnp.sum(p, axis=1, keepdims=True))
            ps.append(_bf(p))
        o = _dot(jnp.concatenate(ps, axis=0), v)
        outs += [o[i * tq:(i + 1) * tq, :] * inv[i] for i in range(DSA_HEAD_GROUP)]
    att = jnp.concatenate(outs, axis=1)
    m_out = _dot(_bf(att), wo_ref[...])
    o_ref[...] = _layer_norm(DEEPNORM_ALPHA * x_ref[...] + m_out, gb_ref[0:1, :], gb_ref[1:2, :])


def _dsa_attn(x2d, batch, seq_len, q, k, v, qi, ki, wi, w_o, ln_g, ln_b):
    n_tok, c = x2d.shape
    tq = Q_TILE
    k_sel = min(TOPK_MAX, seq_len // 4)
    kg = min(KEY_GROUP, seq_len)
    assert seq_len % kg == 0 and kg % tq == 0
    qb_per_group = kg // tq
    qb_per_seq = seq_len // tq
    ki3 = ki.reshape(batch, seq_len, IDX_DIM)
    k3 = k.reshape(batch, seq_len, HEAD_DIM)
    v3 = v.reshape(batch, seq_len, HEAD_DIM)
    gb = jnp.zeros((V7X_SUBLANES, c), F32).at[0].set(ln_g).at[1].set(ln_b)
    wo = _bf(w_o)
    outs = []
    for grp in range(seq_len // kg):
        q0 = grp * qb_per_group
        n_keys = (grp + 1) * kg
        row = lambda w: pl.BlockSpec((tq, w), lambda bi, j: (bi * qb_per_seq + q0 + j, 0))
        per_batch = lambda w: pl.BlockSpec((None, seq_len, w), lambda bi, j: (bi, 0, 0))
        out = pl.pallas_call(
            functools.partial(_dsa_attn_kernel, q_block0=q0, n_keys=n_keys, k_sel=k_sel),
            grid=(batch, qb_per_group),
            in_specs=[row(c), row(q.shape[1]), row(qi.shape[1]), row(wi.shape[1]),
                      per_batch(IDX_DIM), per_batch(HEAD_DIM), per_batch(HEAD_DIM),
                      _full(wo.shape), _full(gb.shape)],
            out_specs=pl.BlockSpec((None, tq, c), lambda bi, j: (bi, j, 0)),
            out_shape=jax.ShapeDtypeStruct((batch, kg, c), F32),
            compiler_params=_params("parallel", "parallel"),
            name=f"dsa_attn_{n_keys}",
        )(x2d, q, qi, wi, ki3, k3, v3, wo, gb)
        outs.append(out)
    return jnp.concatenate(outs, axis=1).reshape(n_tok, c)


def _rope_tables(seq_len):
    pos = jnp.arange(seq_len, dtype=F32)
    inv = 1.0 / (ROPE_THETA ** (jnp.arange(0, HEAD_DIM, 2, dtype=F32) / HEAD_DIM))
    ang = pos[:, None] * inv[None, :]
    cos, sin = jnp.cos(ang), jnp.sin(ang)
    cos2 = jnp.tile(jnp.concatenate([cos, cos], axis=1), (1, V7X_LANES // HEAD_DIM))
    sin2 = jnp.tile(jnp.concatenate([-sin, sin], axis=1), (1, V7X_LANES // HEAD_DIM))
    return cos2, sin2


def kernel(x, rw_mu, rw_w_rkv, rw_w0, rw_w1, rw_w2, rw_a0, rw_a1, rw_a2, rw_g1, rw_g2, rw_k_k, rw_k_a, rw_r_k, rw_lnx_g, rw_lnx_b, rw_w_o, ds_w_in, ds_idx_ln_g, ds_idx_ln_b, ds_w_o, ln_mix_g, ln_mix_b, moe_w_router, moe_b_router, moe_w_gu, moe_b_gu, moe_w_down, moe_b_down, ln_ffn_g, ln_ffn_b):
    batch, seq_len, c = x.shape
    assert c == D_MODEL
    h = x.reshape(batch * seq_len, c)
    cos2, sin2 = _rope_tables(seq_len)
    head_of = jnp.arange(2 * V7X_LANES, dtype=jnp.int32) // HEAD_DIM
    bd = (head_of[:, None] == head_of[None, :]).astype(BF16)
    n_mixers = 2
    for i in range(DEPTH):
        j = i // n_mixers
        if i % n_mixers == 0:
            r, lw, k, v, na, b, g = _rwkv_pre(h, seq_len, rw_mu[j], rw_w_rkv[j], rw_w0[j], rw_w1[j], rw_w2[j],
                                              rw_a0[j], rw_a1[j], rw_a2[j], rw_g1[j], rw_g2[j], rw_k_k[j], rw_k_a[j], bd)
            z = _rwkv_scan(r, lw, k, v, na, b, g, rw_lnx_g[j], rw_lnx_b[j], rw_r_k[j], batch, seq_len)
            h = _proj_ln(h, z, rw_w_o[j], ln_mix_g[i], ln_mix_b[i])
        else:
            q, k, v, qi, ki, wi = _dsa_proj(h, seq_len, ds_w_in[j], ds_idx_ln_g[j], ds_idx_ln_b[j], cos2, sin2)
            h = _dsa_attn(h, batch, seq_len, q, k, v, qi, ki, wi, ds_w_o[j], ln_mix_g[i], ln_mix_b[i])
        h = _moe(h, moe_w_router[i], moe_b_router[i], moe_w_gu, moe_b_gu, moe_w_down, moe_b_down,
                 ln_ffn_g[i], ln_ffn_b[i], i)
    return h.reshape(batch, seq_len, c)
```

```python
import functools
import math

import jax
import jax.numpy as jnp
from jax import lax
from jax.experimental import pallas as pl
from jax.experimental.pallas import tpu as pltpu

D_MODEL = 1024
HEAD_DIM = 64
N_HEADS = D_MODEL // HEAD_DIM
RW_GN_EPS = 64e-5
IDX_HEADS = 8
IDX_DIM = 64
TOPK_MAX = 256
ROPE_THETA = 10000.0
N_EXPERTS = 32
TOP_K = 4
SWIGLU_LIMIT = 7.0
SWIGLU_ALPHA = 1.702
LN_EPS = 1e-5
DEPTH = 2
DEEPNORM_ALPHA = (2.0 * DEPTH) ** 0.25
IDX_W_SCALE = IDX_HEADS ** -0.5 * IDX_DIM ** -0.5

V7X_LANES = 128
V7X_SUBLANES = 8
V7X_VMEM_LIMIT_BYTES = 56 * 1024 * 1024

SCAN_CHUNK = 64
SCAN_SUB = 2
ROW_TILE = 256
Q_TILE = 128
KEY_GROUP = 256
MOE_TILE = 512
MOE_ROUTER_TILE = 1024
DSA_HEAD_GROUP = 4
DSA_TWO_BIT_SEARCH_KEYS = 768
DSA_UNROLLED_SEARCH_KEYS = 2048

F32 = jnp.float32
BF16 = jnp.bfloat16
INT_MIN = -2147483648


def _bf(x):
    return x.astype(BF16)


def _split2(x):
    hi = x.astype(BF16)
    lo = (x - hi.astype(F32)).astype(BF16)
    return hi, lo


def _dot(a, b):
    return jnp.dot(a, b, preferred_element_type=F32)


def _dot_nt(a, b):
    return lax.dot_general(a, b, (((1,), (1,)), ((), ())), preferred_element_type=F32)


def _dot_tn(a, b):
    return lax.dot_general(a, b, (((0,), (0,)), ((), ())), preferred_element_type=F32)


def _layer_norm(h, g, b):
    mu = jnp.mean(h, axis=-1, keepdims=True)
    d = h - mu
    var = jnp.mean(d * d, axis=-1, keepdims=True)
    return d * lax.rsqrt(var + LN_EPS) * g + b


def _sigmoid(z):
    return 1.0 / (1.0 + jnp.exp(-z))


def _params(*sem):
    return pltpu.CompilerParams(dimension_semantics=sem, vmem_limit_bytes=V7X_VMEM_LIMIT_BYTES)


def _full(shape):
    n = len(shape)
    return pl.BlockSpec(shape, lambda *_: (0,) * n)


def _rwkv_pre_kernel(x_ref, xp_ref, mu_ref, wrkv_ref, w1_ref, w2_ref, a1_ref, a2_ref, g1_ref, g2_ref,
                     vec_ref, bd_ref, r_ref, lw_ref, k_ref, v_ref, na_ref, b_ref, g_ref, *, tiles_per_seq):
    i = pl.program_id(0)
    x = x_ref[...]
    prev = xp_ref[V7X_SUBLANES - 1:V7X_SUBLANES, :]
    prev = jnp.where(i % tiles_per_seq == 0, 0.0, prev)
    rows = lax.broadcasted_iota(jnp.int32, x.shape, 0)
    xprev = jnp.where(rows == 0, prev, pltpu.roll(x, 1, 0))
    xx = xprev - x

    def mix(p):
        return x + xx * mu_ref[p:p + 1, :]

    w0 = vec_ref[0:1, :]
    a0 = vec_ref[1:2, :]
    k_k = vec_ref[2:3, :]
    k_a = vec_ref[3:4, :]

    r = _dot(_bf(mix(0)), wrkv_ref[0])
    k = _dot(_bf(mix(1)), wrkv_ref[1])
    v = _dot(_bf(mix(2)), wrkv_ref[2])

    zw = w0 + _dot(_bf(jnp.tanh(_dot(_bf(mix(3)), w1_ref[...]))), w2_ref[...])
    nz = -zw
    softplus = jnp.maximum(nz, 0.0) + jnp.log1p(jnp.exp(-jnp.abs(nz)))
    w_log = -softplus - 0.5
    lw = -jnp.exp(w_log)

    a = _sigmoid(a0 + _dot(_bf(_dot(_bf(mix(4)), a1_ref[...])), a2_ref[...]))
    g = _dot(_bf(_sigmoid(_dot(_bf(mix(5)), g1_ref[...]))), g2_ref[...])

    kk = k * k_k
    sq_hi, sq_lo = _split2(kk * kk)
    gw = bd_ref.shape[0]
    ss = jnp.concatenate([_dot(sq_hi[:, c0:c0 + gw], bd_ref[...]) + _dot(sq_lo[:, c0:c0 + gw], bd_ref[...])
                          for c0 in range(0, kk.shape[1], gw)], axis=1)
    kk = kk / jnp.maximum(jnp.sqrt(ss), 1e-12)
    k = k * (1.0 + (a - 1.0) * k_a)

    r_ref[...] = r
    lw_ref[...] = lw
    k_ref[...] = k
    v_ref[...] = v
    na_ref[...] = -kk
    b_ref[...] = kk * a
    g_ref[...] = g


def _rwkv_pre(x2d, seq_len, mu, w_rkv, w0, w1, w2, a0, a1, a2, g1, g2, k_k, k_a, bd):
    n_tok, c = x2d.shape
    tm = min(ROW_TILE, seq_len)
    assert seq_len % tm == 0 and tm % V7X_SUBLANES == 0
    vecs = jnp.zeros((V7X_SUBLANES, c), F32).at[0].set(w0).at[1].set(a0).at[2].set(k_k).at[3].set(k_a)
    mu8 = jnp.zeros((V7X_SUBLANES, c), F32).at[:6].set(mu)
    row = pl.BlockSpec((tm, c), lambda i: (i, 0))
    sub_per_tile = tm // V7X_SUBLANES
    prev = pl.BlockSpec((V7X_SUBLANES, c), lambda i: (jnp.maximum(i * sub_per_tile - 1, 0), 0))
    ws = [_bf(w_rkv), _bf(w1), _bf(w2), _bf(a1), _bf(a2), _bf(g1), _bf(g2)]
    out = jax.ShapeDtypeStruct((n_tok, c), F32)
    return pl.pallas_call(
        functools.partial(_rwkv_pre_kernel, tiles_per_seq=seq_len // tm),
        grid=(n_tok // tm,),
        in_specs=[row, prev, _full(mu8.shape)] + [_full(w.shape) for w in ws] + [_full(vecs.shape), _full(bd.shape)],
        out_specs=[row] * 7,
        out_shape=[out] * 7,
        compiler_params=_params("parallel"),
        name="rwkv_pre",
    )(x2d, x2d, mu8, *ws, vecs, bd)


def _rwkv_scan_kernel(r_ref, lw_ref, k_ref, v_ref, na_ref, b_ref, g_ref, vec_ref, o_ref, s_ref):
    R, C = r_ref.shape
    L = SCAN_CHUNK
    N = HEAD_DIM
    PW = 2 * N
    assert PW == V7X_LANES and 2 * L == PW and R % L == 0
    subs = range(R // L)
    rs = [slice(c * L, (c + 1) * L) for c in subs]

    @pl.when(pl.program_id(1) == 0)
    def _():
        s_ref[...] = jnp.zeros_like(s_ref)

    def head_a(width, n_rows=L):
        return (lax.broadcasted_iota(jnp.int32, (n_rows, width), 1) & N) == 0

    def split_ab(z):
        m = head_a(z.shape[1])
        return jnp.concatenate([jnp.where(m, z, 0.0), jnp.where(m, 0.0, z)], axis=0)

    def split_ba(z):
        m = head_a(z.shape[1])
        return jnp.concatenate([jnp.where(m, 0.0, z), jnp.where(m, z, 0.0)], axis=0)

    rows = lax.broadcasted_iota(jnp.int32, (R, R), 0)
    cols = lax.broadcasted_iota(jnp.int32, (R, R), 1)
    lw = lw_ref[...]
    tri = jnp.where(jnp.logical_and(rows >= cols, (rows ^ cols) < L), 1.0, 0.0).astype(BF16)
    h1 = lw.astype(BF16)
    r1 = lw - h1.astype(F32)
    h2 = r1.astype(BF16)
    h3 = (r1 - h2.astype(F32)).astype(BF16)
    cs = _dot(tri, h1) + _dot(tri, h2) + _dot(tri, h3)
    cs_end = [cs[(c + 1) * L - 1:(c + 1) * L, :] for c in subs]
    g_incl = jnp.exp(cs)
    g_excl = jnp.exp(cs - lw)
    g_inv = jnp.exp(-cs)
    g_tail = jnp.exp(jnp.concatenate([jnp.broadcast_to(e, (L, C)) for e in cs_end], axis=0) - cs)
    g_last = [jnp.exp(e) for e in cs_end]

    r = r_ref[...]
    k = k_ref[...]
    v = v_ref[...]
    b = b_ref[...]
    at = na_ref[...] * g_excl
    rt = r * g_incl
    bt = b * g_inv
    kt = k * g_inv
    bh = b * g_tail
    kh = k * g_tail

    n_pairs = C // PW
    units = range(len(subs) * n_pairs)
    sls = [(rs[i // n_pairs], slice((i % n_pairs) * PW, (i % n_pairs + 1) * PW)) for i in units]
    pairs = units
    t_idx = lax.broadcasted_iota(jnp.int32, (L, PW), 0)
    s_idx = lax.broadcasted_iota(jnp.int32, (L, PW), 1) & (N - 1)
    strict2 = t_idx > s_idx
    incl2 = t_idx >= s_idx
    is_a = head_a(PW)
    is_a2 = head_a(PW, 2 * L)

    lhs = [jnp.concatenate([at[sl], rt[sl]], axis=0) for sl in sls]
    g_a = [_dot_nt(_bf(jnp.where(is_a2, lhs[j], 0.0)), _bf(jnp.concatenate([bt[sls[j]], kt[sls[j]]], axis=0)))
           for j in pairs]
    g_b = [_dot_nt(_bf(jnp.where(is_a2, 0.0, lhs[j])), _bf(jnp.concatenate([kt[sls[j]], bt[sls[j]]], axis=0)))
           for j in pairs]
    pp = [jnp.where(strict2, jnp.where(is_a, g_a[j][:L], g_b[j][:L]), 0.0) for j in pairs]
    m_ak = [jnp.where(strict2, jnp.where(is_a, g_b[j][:L], g_a[j][:L]), 0.0) for j in pairs]
    w_rb = [jnp.where(incl2, jnp.where(is_a, g_a[j][L:], g_b[j][L:]), 0.0) for j in pairs]
    w_rk = [jnp.where(incl2, jnp.where(is_a, g_b[j][L:], g_a[j][L:]), 0.0) for j in pairs]
    mv = [_dot(_bf(m_ak[j]), _bf(split_ba(v[sls[j]]))) for j in pairs]
    xs = [jnp.concatenate([at[sls[j]], mv[j]], axis=1) for j in pairs]
    levels = int(math.log2(L))
    for lvl in range(levels):
        if lvl < levels - 1:
            z = [_dot(_bf(pp[j]), _bf(split_ab(jnp.concatenate([xs[j], pp[j]], axis=1)))) for j in pairs]
            xs = [xs[j] + z[j][:, :2 * PW] for j in pairs]
            pp = [z[j][:, 2 * PW:] for j in pairs]
        else:
            xs = [xs[j] + _dot(_bf(pp[j]), _bf(split_ab(xs[j]))) for j in pairs]
    zy = [_dot(_bf(jnp.concatenate([w_rb[j], w_rk[j]], axis=1)),
               _bf(jnp.concatenate([split_ab(xs[j]),
                                    split_ba(jnp.concatenate([jnp.zeros((L, PW), F32), v[sls[j]]], axis=1))], axis=0)))
          for j in pairs]
    vr = lax.broadcasted_iota(jnp.int32, (PW, PW), 0)
    kc = lax.broadcasted_iota(jnp.int32, (PW, PW), 1)
    same_head = (vr & N) == (kc & N)
    state = [s_ref[p] for p in range(n_pairs)]
    y_rows = []
    for c in subs:
        ids = [c * n_pairs + p for p in range(n_pairs)]
        uy = [_dot_nt(_bf(jnp.concatenate([xs[j][:, :PW], rt[sls[j]] + zy[j][:, :PW]], axis=0)), _bf(state[j % n_pairs]))
              for j in ids]
        u = [uy[p][:L] + xs[j][:, PW:] for p, j in enumerate(ids)]
        y_rows.append(jnp.concatenate([uy[p][L:] + zy[j][:, PW:] for p, j in enumerate(ids)], axis=1))
        upd = [_dot_tn(_bf(jnp.concatenate([u[p], v[sls[j]]], axis=0)),
                       _bf(jnp.concatenate([bh[sls[j]], kh[sls[j]]], axis=0))) for p, j in enumerate(ids)]
        state = [state[p] * g_last[c][:, sls[j][1]] + jnp.where(same_head, upd[p], 0.0) for p, j in enumerate(ids)]
    for p in range(n_pairs):
        s_ref[p] = state[p]

    yy = jnp.concatenate(y_rows, axis=0)
    gw = 2 * PW
    jr = lax.broadcasted_iota(jnp.int32, (gw, gw), 0)
    jc = lax.broadcasted_iota(jnp.int32, (gw, gw), 1)
    ones_bd = jnp.where((jr ^ jc) < N, 1.0, 0.0).astype(BF16)

    def head_sum(z):
        hi, lo = _split2(z)
        return jnp.concatenate([_dot(hi[:, c0:c0 + gw], ones_bd) + _dot(lo[:, c0:c0 + gw], ones_bd)
                                for c0 in range(0, C, gw)], axis=1)

    yd = yy - head_sum(yy) * (1.0 / N)
    yv = head_sum(yd * yd) * (1.0 / N)
    yn = yd * lax.rsqrt(yv + RW_GN_EPS) * vec_ref[0:1, :] + vec_ref[1:2, :]
    bonus = head_sum(r * k * vec_ref[2:3, :]) * v
    o_ref[...] = (yn + bonus) * g_ref[...]


def _rwkv_scan(r, lw, k, v, na, b, g, lnx_g, lnx_b, r_k, batch, seq_len):
    n_tok, c = r.shape
    rows = SCAN_CHUNK * SCAN_SUB
    assert seq_len % rows == 0
    vecs = jnp.zeros((V7X_SUBLANES, c), F32).at[0].set(lnx_g).at[1].set(lnx_b).at[2].set(r_k.reshape(c))
    chunks = seq_len // rows
    blk = pl.BlockSpec((rows, c), lambda bi, ci: (bi * chunks + ci, 0))
    return pl.pallas_call(
        _rwkv_scan_kernel,
        grid=(batch, chunks),
        in_specs=[blk] * 7 + [_full(vecs.shape)],
        out_specs=blk,
        out_shape=jax.ShapeDtypeStruct((n_tok, c), F32),
        scratch_shapes=[pltpu.VMEM((c // V7X_LANES, V7X_LANES, V7X_LANES), F32)],
        compiler_params=_params("parallel", "arbitrary"),
        name="rwkv_scan",
    )(r, lw, k, v, na, b, g, vecs)


def _proj_ln_kernel(x_ref, z_ref, w_ref, gb_ref, o_ref):
    m = _dot(_bf(z_ref[...]), w_ref[...])
    o_ref[...] = _layer_norm(DEEPNORM_ALPHA * x_ref[...] + m, gb_ref[0:1, :], gb_ref[1:2, :])


def _proj_ln(x2d, z2d, w, ln_g, ln_b):
    n_tok, c = x2d.shape
    tm = min(2 * ROW_TILE, n_tok)
    gb = jnp.zeros((V7X_SUBLANES, c), F32).at[0].set(ln_g).at[1].set(ln_b)
    row = pl.BlockSpec((tm, c), lambda i: (i, 0))
    return pl.pallas_call(
        _proj_ln_kernel,
        grid=(n_tok // tm,),
        in_specs=[row, row, _full(w.shape), _full(gb.shape)],
        out_specs=row,
        out_shape=jax.ShapeDtypeStruct((n_tok, c), F32),
        compiler_params=_params("parallel"),
        name="proj_ln",
    )(x2d, z2d, _bf(w), gb)


def _router_kernel(x_ref, wt_ref, b_ref, earlier_ref, e_ref, gate_ref, rank_ref, cnt_ref):
    x_hi, x_lo = _split2(x_ref[...])
    w_hi, w_lo = _split2(wt_ref[...])
    logits = _dot_nt(w_hi, x_hi) + _dot_nt(w_hi, x_lo) + _dot_nt(w_lo, x_hi) + b_ref[...]
    n_e = logits.shape[0]
    eid = lax.broadcasted_iota(jnp.int32, logits.shape, 0)
    tops, idxs, hots = [], [], []
    for _ in range(TOP_K):
        m = jnp.max(logits, axis=0, keepdims=True)
        idx = jnp.min(jnp.where(logits == m, eid, n_e), axis=0, keepdims=True)
        tops.append(m)
        idxs.append(idx)
        hots.append(eid == idx)
        logits = jnp.where(eid == idx, -jnp.inf, logits)
    ex = [jnp.exp(t - tops[0]) for t in tops]
    denom = ex[0] + ex[1] + ex[2] + ex[3]
    e_ref[...] = jnp.concatenate(idxs, axis=0)
    gate_ref[...] = jnp.concatenate([e / denom for e in ex], axis=0)
    chosen = sum(jnp.where(hot, 1.0, 0.0) for hot in hots)
    before = _dot(_bf(chosen), earlier_ref[...])
    rank_ref[...] = jnp.concatenate([jnp.sum(jnp.where(hot, before, 0.0), axis=0, keepdims=True) for hot in hots],
                                    axis=0).astype(jnp.int32)
    cnt_ref[...] = jnp.broadcast_to(jnp.sum(chosen, axis=1, keepdims=True), cnt_ref.shape).astype(jnp.int32)


def _router(x2d, w_router, b_router):
    n_tok, c = x2d.shape
    tm = min(MOE_ROUTER_TILE, n_tok)
    wt = w_router.T
    bcol = b_router.reshape(N_EXPERTS, 1)
    tok = jnp.arange(tm, dtype=jnp.int32)
    earlier = (tok[:, None] < tok[None, :]).astype(BF16)
    tile_of = lambda i: (0, i)
    return pl.pallas_call(
        _router_kernel,
        grid=(n_tok // tm,),
        in_specs=[pl.BlockSpec((tm, c), lambda i: (i, 0)), _full(wt.shape), _full(bcol.shape), _full(earlier.shape)],
        out_specs=[pl.BlockSpec((TOP_K, tm), tile_of)] * 3 + [pl.BlockSpec((None, N_EXPERTS, V7X_LANES), lambda i: (i, 0, 0))],
        out_shape=[jax.ShapeDtypeStruct((TOP_K, n_tok), jnp.int32), jax.ShapeDtypeStruct((TOP_K, n_tok), F32),
                   jax.ShapeDtypeStruct((TOP_K, n_tok), jnp.int32),
                   jax.ShapeDtypeStruct((n_tok // tm, N_EXPERTS, V7X_LANES), jnp.int32)],
        compiler_params=_params("parallel"),
        name="moe_router",
    )(x2d, wt, bcol, earlier)


def _expert_kernel(blk_e_ref, n_used_ref, x_ref, wgu_ref, bgu_ref, wd_ref, bd_ref, y_ref, wgu_bf, wd_bf):
    i = pl.program_id(0)
    e = blk_e_ref[i]
    changed = jnp.logical_or(i == 0, e != blk_e_ref[jnp.maximum(i - 1, 0)])

    @pl.when(changed)
    def _():
        wgu_bf[...] = _bf(wgu_ref[...])
        wd_bf[...] = _bf(wd_ref[...])

    @pl.when(i < n_used_ref[0])
    def _():
        f = wd_ref.shape[0]
        gu = _dot(_bf(x_ref[...]), wgu_bf[...]) + bgu_ref[...]
        glu = jnp.minimum(gu[:, :f], SWIGLU_LIMIT)
        lin = jnp.clip(gu[:, f:], -SWIGLU_LIMIT, SWIGLU_LIMIT)
        act = glu * _sigmoid(SWIGLU_ALPHA * glu) * (lin + 1.0)
        y_ref[...] = _dot(_bf(act), wd_bf[...]) + bd_ref[...]

    @pl.when(i >= n_used_ref[0])
    def _():
        y_ref[...] = jnp.zeros_like(y_ref)


def _experts(x_rows, blk_e, n_used, w_gu, b_gu, w_down, b_down, layer):
    n_rows, c = x_rows.shape
    n_l, n_e, _, f2 = w_gu.shape
    f = f2 // 2
    tb = MOE_TILE
    grid_spec = pltpu.PrefetchScalarGridSpec(
        num_scalar_prefetch=2,
        grid=(n_rows // tb,),
        in_specs=[
            pl.BlockSpec((tb, c), lambda i, be, nu: (jnp.minimum(i, nu[0] - 1), 0)),
            pl.BlockSpec((None, None, c, f2), lambda i, be, nu: (layer, be[i], 0, 0)),
            pl.BlockSpec((None, None, 1, f2), lambda i, be, nu: (layer, be[i], 0, 0)),
            pl.BlockSpec((None, None, f, c), lambda i, be, nu: (layer, be[i], 0, 0)),
            pl.BlockSpec((None, None, 1, c), lambda i, be, nu: (layer, be[i], 0, 0)),
        ],
        out_specs=pl.BlockSpec((tb, c), lambda i, be, nu: (i, 0)),
        scratch_shapes=[pltpu.VMEM((c, f2), BF16), pltpu.VMEM((f, c), BF16)],
    )
    return pl.pallas_call(
        _expert_kernel,
        grid_spec=grid_spec,
        out_shape=jax.ShapeDtypeStruct((n_rows, c), F32),
        compiler_params=_params("arbitrary"),
        name="moe_experts",
    )(blk_e, n_used, x_rows, w_gu, b_gu.reshape(n_l, n_e, 1, f2), w_down, b_down.reshape(n_l, n_e, 1, c))


def _combine_ln_kernel(x_ref, y_ref, gate_ref, gb_ref, o_ref):
    acc = DEEPNORM_ALPHA * x_ref[...]
    for j in range(TOP_K):
        acc = acc + gate_ref[:, j:j + 1] * y_ref[j]
    o_ref[...] = _layer_norm(acc, gb_ref[0:1, :], gb_ref[1:2, :])


def _combine_ln(x2d, y4, gates, ln_g, ln_b):
    n_tok, c = x2d.shape
    tm = min(ROW_TILE, n_tok)
    gb = jnp.zeros((V7X_SUBLANES, c), F32).at[0].set(ln_g).at[1].set(ln_b)
    return pl.pallas_call(
        _combine_ln_kernel,
        grid=(n_tok // tm,),
        in_specs=[pl.BlockSpec((tm, c), lambda i: (i, 0)), pl.BlockSpec((TOP_K, tm, c), lambda i: (0, i, 0)),
                  pl.BlockSpec((tm, TOP_K), lambda i: (i, 0)), _full(gb.shape)],
        out_specs=pl.BlockSpec((tm, c), lambda i: (i, 0)),
        out_shape=jax.ShapeDtypeStruct((n_tok, c), F32),
        compiler_params=_params("parallel"),
        name="moe_combine_ln",
    )(x2d, y4, gates, gb)


def _moe(x2d, w_router, b_router, w_gu, b_gu, w_down, b_down, ln_g, ln_b, layer):
    n_tok, c = x2d.shape
    top_e, gates, rank, tile_cnt = _router(x2d, w_router, b_router)
    tb = MOE_TILE
    tile_cnt = tile_cnt[:, :, 0]
    n_tiles = tile_cnt.shape[0]
    counts = jnp.sum(tile_cnt, axis=0)
    padded = (counts + tb - 1) // tb * tb
    pad_end = jnp.cumsum(padded)
    pad_start = pad_end - padded
    base = pad_start[None, :] + jnp.cumsum(tile_cnt, axis=0) - tile_cnt
    base_tok = jnp.broadcast_to(base[:, None, :], (n_tiles, n_tok // n_tiles, N_EXPERTS)).reshape(n_tok, N_EXPERTS)
    hot = top_e[:, :, None] == jnp.arange(N_EXPERTS, dtype=jnp.int32)
    dest = jnp.sum(jnp.where(hot, base_tok[None], 0), axis=-1) + rank
    n_rows = -(-(n_tok * TOP_K + N_EXPERTS * (tb - 1)) // tb) * tb
    blk_start = jnp.arange(n_rows // tb, dtype=jnp.int32) * tb
    blk_e = jnp.minimum(jnp.sum((pad_end[None, :] <= blk_start[:, None]).astype(jnp.int32), axis=1), N_EXPERTS - 1)
    n_used = (pad_end[-1] // tb).astype(jnp.int32).reshape(1)
    tok = jnp.broadcast_to(jnp.arange(n_tok, dtype=jnp.int32)[None, :], (TOP_K, n_tok))
    rows_tok = jnp.zeros((n_rows,), jnp.int32).at[dest.reshape(-1)].set(
        tok.reshape(-1), unique_indices=True, mode="promise_in_bounds")
    x_rows = x2d.at[rows_tok].get(mode="promise_in_bounds")
    y_rows = _experts(x_rows, blk_e, n_used, w_gu, b_gu, w_down, b_down, layer)
    y4 = y_rows.at[dest.reshape(-1)].get(unique_indices=True, mode="promise_in_bounds").reshape(TOP_K, n_tok, c)
    return _combine_ln(x2d, y4, gates.T, ln_g, ln_b)


def _rope(z, cos, sin_signed):
    half = HEAD_DIM // 2
    width = z.shape[-1]
    lane = lax.broadcasted_iota(jnp.int32, z.shape, z.ndim - 1)
    swapped = jnp.where(lane % HEAD_DIM < half, pltpu.roll(z, width - half, z.ndim - 1), pltpu.roll(z, half, z.ndim - 1))
    return z * cos + swapped * sin_signed


def _dsa_proj_kernel(x_ref, w_ref, cos_ref, sin_ref, gb_ref, q_ref, k_ref, v_ref, qi_ref, ki_ref, wi_ref):
    c = x_ref.shape[1]
    proj = _dot(_bf(x_ref[...]), w_ref[...])
    cos2 = cos_ref[...]
    sin2 = sin_ref[...]
    q_w = q_ref.shape[1]
    qi_w = qi_ref.shape[1]
    q = proj[:, :q_w]
    q_ref[...] = _bf(_rope(q, jnp.tile(cos2, (1, q_w // V7X_LANES)), jnp.tile(sin2, (1, q_w // V7X_LANES))))
    kv = proj[:, q_w:q_w + V7X_LANES]
    k_ref[...] = _bf(_rope(kv, cos2, sin2)[:, :HEAD_DIM])
    v_ref[...] = _bf(kv[:, HEAD_DIM:])
    o = q_w + V7X_LANES
    qi = proj[:, o:o + qi_w]
    qi_ref[...] = _rope(qi, jnp.tile(cos2, (1, qi_w // V7X_LANES)), jnp.tile(sin2, (1, qi_w // V7X_LANES)))
    o = o + qi_w
    kiw = proj[:, o:o + V7X_LANES]
    in_ki = lax.broadcasted_iota(jnp.int32, kiw.shape, 1) < IDX_DIM
    mu = jnp.sum(kiw, axis=-1, keepdims=True) * (1.0 / IDX_DIM)
    d = jnp.where(in_ki, kiw - mu, 0.0)
    var = jnp.sum(d * d, axis=-1, keepdims=True) * (1.0 / IDX_DIM)
    ki = d * lax.rsqrt(var + LN_EPS) * gb_ref[0:1, :] + gb_ref[1:2, :]
    ki_ref[...] = _rope(ki, cos2, sin2)[:, :IDX_DIM]
    o = o + V7X_LANES
    wi_ref[...] = proj[:, o:o + IDX_HEADS] * IDX_W_SCALE


def _dsa_proj(x2d, seq_len, w_in, idx_g, idx_b, cos2, sin2):
    n_tok, c = x2d.shape
    q_w = N_HEADS * HEAD_DIM
    qi_w = IDX_HEADS * IDX_DIM
    o_ki = q_w + 2 * HEAD_DIM + qi_w
    w_pad = jnp.concatenate([w_in[:, :o_ki + IDX_DIM], jnp.zeros((c, V7X_LANES - IDX_DIM), F32),
                             w_in[:, o_ki + IDX_DIM:], jnp.zeros((c, V7X_LANES - IDX_HEADS), F32)], axis=1)
    tm = min(ROW_TILE, seq_len)
    tiles = seq_len // tm
    gb = jnp.zeros((V7X_SUBLANES, V7X_LANES), F32).at[0, :IDX_DIM].set(idx_g).at[1, :IDX_DIM].set(idx_b)
    row = lambda w: pl.BlockSpec((tm, w), lambda i: (i, 0))
    tab = pl.BlockSpec((tm, V7X_LANES), lambda i: (i % tiles, 0))
    shapes = [(q_w, BF16), (HEAD_DIM, BF16), (HEAD_DIM, BF16), (qi_w, F32), (IDX_DIM, F32), (IDX_HEADS, F32)]
    return pl.pallas_call(
        _dsa_proj_kernel,
        grid=(n_tok // tm,),
        in_specs=[row(c), _full(w_pad.shape), tab, tab, _full(gb.shape)],
        out_specs=[row(w) for w, _ in shapes],
        out_shape=[jax.ShapeDtypeStruct((n_tok, w), dt) for w, dt in shapes],
        compiler_params=_params("parallel"),
        name="dsa_proj",
    )(x2d, _bf(w_pad), cos2, sin2, gb)


def _dsa_attn_kernel(x_ref, q_ref, qi_ref, wi_ref, ki_ref, k_ref, v_ref, wo_ref, gb_ref, o_ref, *, q_block0, n_keys, k_sel):
    qb = q_block0 + pl.program_id(1)
    tq = q_ref.shape[0]
    nk = n_keys

    ki_hi, ki_lo = _split2(ki_ref[0:nk, :])
    ki3 = jnp.concatenate([ki_hi, ki_hi, ki_lo], axis=1)
    qi = qi_ref[...]
    wi = wi_ref[...]
    s = jnp.zeros((tq, nk), F32)
    for h in range(IDX_HEADS):
        q_hi, q_lo = _split2(qi[:, h * IDX_DIM:(h + 1) * IDX_DIM])
        sh = _dot_nt(jnp.concatenate([q_hi, q_lo, q_hi], axis=1), ki3)
        s = s + jnp.maximum(sh, 0.0) * wi[:, h:h + 1]

    key_pos = lax.broadcasted_iota(jnp.int32, (tq, nk), 1)
    q_pos = qb * tq + lax.broadcasted_iota(jnp.int32, (tq, nk), 0)
    causal = key_pos <= q_pos

    bits = pltpu.bitcast(s, jnp.int32)
    keys = jnp.where(causal, bits ^ ((bits >> 31) & jnp.int32(0x7FFFFFFF)), jnp.int32(INT_MIN))

    step_bits = 2 if nk <= DSA_TWO_BIT_SEARCH_KEYS else 1

    def search_step(i, tau):
        shift = 32 - step_bits * (i + 1)
        settled = jnp.zeros((tq, 1), jnp.int32)
        for c in range(1, 1 << step_bits):
            cand = tau | jnp.left_shift(jnp.int32(c), shift)
            cnt = jnp.sum(jnp.where(keys >= (cand ^ jnp.int32(INT_MIN)), 1.0, 0.0), axis=1, keepdims=True)
            settled = settled + jnp.where(cnt >= k_sel, 1, 0)
        return tau | jnp.left_shift(settled, shift)

    tau = lax.fori_loop(0, 32 // step_bits, search_step, jnp.zeros((tq, 1), jnp.int32),
                        unroll=nk <= DSA_UNROLLED_SEARCH_KEYS) ^ jnp.int32(INT_MIN)
    above = keys > tau
    tied = keys == tau
    room = k_sel - jnp.sum(jnp.where(above, 1.0, 0.0), axis=1, keepdims=True)
    tied_bf = jnp.where(tied, 1.0, 0.0).astype(BF16)
    cr = lax.broadcasted_iota(jnp.int32, (V7X_LANES, V7X_LANES), 0)
    cc = lax.broadcasted_iota(jnp.int32, (V7X_LANES, V7X_LANES), 1)
    earlier_bf = jnp.where(cr < cc, 1.0, 0.0).astype(BF16)
    carry = jnp.zeros((tq, 1), F32)
    before = []
    for ch in range(nk // V7X_LANES):
        t = tied_bf[:, ch * V7X_LANES:(ch + 1) * V7X_LANES]
        before.append(_dot(t, earlier_bf) + carry)
        carry = carry + jnp.sum(t.astype(F32), axis=1, keepdims=True)
    before = jnp.concatenate(before, axis=1)
    sel = jnp.logical_and(jnp.logical_or(above, jnp.logical_and(tied, before < room)), causal)
    bias = jnp.where(sel, 0.0, -1e30)

    k = k_ref[0:nk, :]
    v = v_ref[0:nk, :]
    q = q_ref[...] * (HEAD_DIM ** -0.5)
    n_heads = q.shape[1] // HEAD_DIM
    outs = []
    for g0 in range(0, n_heads, DSA_HEAD_GROUP):
        hs = range(g0, g0 + DSA_HEAD_GROUP)
        logits = _dot_nt(jnp.concatenate([q[:, h * HEAD_DIM:(h + 1) * HEAD_DIM] for h in hs], axis=0), k)
        ps, inv = [], []
        for i in range(DSA_HEAD_GROUP):
            lg = logits[i * tq:(i + 1) * tq, :] + bias
            p = jnp.exp(lg - jnp.max(lg, axis=1, keepdims=True))
            inv.append(1.0 / jnp.sum(p, axis=1, keepdims=True))
            ps.append(_bf(p))
        o = _dot(jnp.concatenate(ps, axis=0), v)
        outs += [o[i * tq:(i + 1) * tq, :] * inv[i] for i in range(DSA_HEAD_GROUP)]
    att = jnp.concatenate(outs, axis=1)
    m_out = _dot(_bf(att), wo_ref[...])
    o_ref[...] = _layer_norm(DEEPNORM_ALPHA * x_ref[...] + m_out, gb_ref[0:1, :], gb_ref[1:2, :])


def _dsa_attn(x2d, batch, seq_len, q, k, v, qi, ki, wi, w_o, ln_g, ln_b):
    n_tok, c = x2d.shape
    tq = Q_TILE
    k_sel = min(TOPK_MAX, seq_len // 4)
    kg = min(KEY_GROUP, seq_len)
    assert seq_len % kg == 0 and kg % tq == 0
    qb_per_group = kg // tq
    qb_per_seq = seq_len // tq
    ki3 = ki.reshape(batch, seq_len, IDX_DIM)
    k3 = k.reshape(batch, seq_len, HEAD_DIM)
    v3 = v.reshape(batch, seq_len, HEAD_DIM)
    gb = jnp.zeros((V7X_SUBLANES, c), F32).at[0].set(ln_g).at[1].set(ln_b)
    wo = _bf(w_o)
    outs = []
    for grp in range(seq_len // kg):
        q0 = grp * qb_per_group
        n_keys = (grp + 1) * kg
        row = lambda w: pl.BlockSpec((tq, w), lambda bi, j: (bi * qb_per_seq + q0 + j, 0))
        per_batch = lambda w: pl.BlockSpec((None, seq_len, w), lambda bi, j: (bi, 0, 0))
        out = pl.pallas_call(
            functools.partial(_dsa_attn_kernel, q_block0=q0, n_keys=n_keys, k_sel=k_sel),
            grid=(batch, qb_per_group),
            in_specs=[row(c), row(q.shape[1]), row(qi.shape[1]), row(wi.shape[1]),
                      per_batch(IDX_DIM), per_batch(HEAD_DIM), per_batch(HEAD_DIM),
                      _full(wo.shape), _full(gb.shape)],
            out_specs=pl.BlockSpec((None, tq, c), lambda bi, j: (bi, j, 0)),
            out_shape=jax.ShapeDtypeStruct((batch, kg, c), F32),
            compiler_params=_params("parallel", "parallel"),
            name=f"dsa_attn_{n_keys}",
        )(x2d, q, qi, wi, ki3, k3, v3, wo, gb)
        outs.append(out)
    return jnp.concatenate(outs, axis=1).reshape(n_tok, c)


def _rope_tables(seq_len):
    pos = jnp.arange(seq_len, dtype=F32)
    inv = 1.0 / (ROPE_THETA ** (jnp.arange(0, HEAD_DIM, 2, dtype=F32) / HEAD_DIM))
    ang = pos[:, None] * inv[None, :]
    cos, sin = jnp.cos(ang), jnp.sin(ang)
    cos2 = jnp.tile(jnp.concatenate([cos, cos], axis=1), (1, V7X_LANES // HEAD_DIM))
    sin2 = jnp.tile(jnp.concatenate([-sin, sin], axis=1), (1, V7X_LANES // HEAD_DIM))
    return cos2, sin2


def kernel(x, rw_mu, rw_w_rkv, rw_w0, rw_w1, rw_w2, rw_a0, rw_a1, rw_a2, rw_g1, rw_g2, rw_k_k, rw_k_a, rw_r_k, rw_lnx_g, rw_lnx_b, rw_w_o, ds_w_in, ds_idx_ln_g, ds_idx_ln_b, ds_w_o, ln_mix_g, ln_mix_b, moe_w_router, moe_b_router, moe_w_gu, moe_b_gu, moe_w_down, moe_b_down, ln_ffn_g, ln_ffn_b):
    batch, seq_len, c = x.shape
    assert c == D_MODEL
    h = x.reshape(batch * seq_len, c)
    cos2, sin2 = _rope_tables(seq_len)
    head_of = jnp.arange(2 * V7X_LANES, dtype=jnp.int32) // HEAD_DIM
    bd = (head_of[:, None] == head_of[None, :]).astype(BF16)
    n_mixers = 2
    for i in range(DEPTH):
        j = i // n_mixers
        if i % n_mixers == 0:
            r, lw, k, v, na, b, g = _rwkv_pre(h, seq_len, rw_mu[j], rw_w_rkv[j], rw_w0[j], rw_w1[j], rw_w2[j],
                                              rw_a0[j], rw_a1[j], rw_a2[j], rw_g1[j], rw_g2[j], rw_k_k[j], rw_k_a[j], bd)
            z = _rwkv_scan(r, lw, k, v, na, b, g, rw_lnx_g[j], rw_lnx_b[j], rw_r_k[j], batch, seq_len)
            h = _proj_ln(h, z, rw_w_o[j], ln_mix_g[i], ln_mix_b[i])
        else:
            q, k, v, qi, ki, wi = _dsa_proj(h, seq_len, ds_w_in[j], ds_idx_ln_g[j], ds_idx_ln_b[j], cos2, sin2)
            h = _dsa_attn(h, batch, seq_len, q, k, v, qi, ki, wi, ds_w_o[j], ln_mix_g[i], ln_mix_b[i])
        h = _moe(h, moe_w_router[i], moe_b_router[i], moe_w_gu, moe_b_gu, moe_w_down, moe_b_down,
                 ln_ffn_g[i], ln_ffn_b[i], i)
    return h.reshape(batch, seq_len, c)
```
